```python
import math
import numpy as np
import jax
import jax.numpy as jnp
from jax import lax

D_MODEL = 2048
BATCH = 2
SEQ = 8192
DEPTH = 4

PLE_DIM = 256
D_FF = 5632
HALF_STEP = 0.5
NORM_EPS = 1e-6
CHUNK = 128
N_BRANCH = 3
A_HEADS = 4
A_QK = 128
A_V = 256
B_HEADS = 16
B_HEADDIM = 64
B_INNER = B_HEADS * B_HEADDIM
B_STATE = 128
B_GROUPS = 2
CONV_K = 5
DT_MIN = 1e-3
DT_MAX = 1e-1
C_HEADS = 8
C_Q_RANK = 512
C_KV_RANK = 512
C_NOPE = 128
C_ROPE = 64
C_V = 128
ROPE_THETA = 10000.0
MAX_POS_OFFSET = 1024

BRANCH_W = 1024
A_QK_W = A_HEADS * A_QK
A_V_W = A_HEADS * A_V
B_XBC_W = B_INNER + 2 * B_GROUPS * B_STATE
IN_SPLITS = (A_QK_W, A_QK_W, A_V_W, A_V_W, 2 * A_HEADS, 2 * A_HEADS,
             B_INNER, B_XBC_W, 2 * B_HEADS,
             C_Q_RANK, C_KV_RANK, C_ROPE,
             N_BRANCH * D_MODEL)
D_IN = sum(IN_SPLITS)

kernel_name = "hybrid_mlstm_ssd_mla_macaron_encoder"


def rmsnorm(x, g):
    xf = x.astype(jnp.float32)
    y = xf * lax.rsqrt(jnp.mean(xf * xf, axis=-1, keepdims=True) + NORM_EPS)
    return (y * g.astype(jnp.float32)).astype(x.dtype)


def swiglu(x, w13, w2):
    a, b = jnp.split(x @ w13, 2, axis=-1)
    return (jax.nn.silu(a) * b) @ w2


def split_cols(t, sizes):
    idx = [int(v) for v in np.cumsum(sizes)[:-1]]
    return jnp.split(t, idx, axis=-1)


def mlstm_chunkwise(q, k, v, li, lf):
    bsz, nh, seq, dk = q.shape
    dv = v.shape[-1]
    nc = seq // CHUNK
    q = q.reshape(bsz, nh, nc, CHUNK, dk)
    k = k.reshape(bsz, nh, nc, CHUNK, dk)
    v = v.reshape(bsz, nh, nc, CHUNK, dv)
    li = li.reshape(bsz, nh, nc, CHUNK)
    lf = lf.reshape(bsz, nh, nc, CHUNK)
    b = jnp.cumsum(lf, axis=-1)
    g = b[..., -1]
    w_state = g[..., None] - b + li
    m_loc = jnp.max(w_state, axis=-1)
    e_state = jnp.exp(w_state - m_loc[..., None])
    c_loc = jnp.einsum('bhcsk,bhcsv->bhckv', k * e_state[..., None], v)
    n_loc = jnp.einsum('bhcs,bhcsk->bhck', e_state, k)

    def step(carry, inp):
        c_st, n_st, m_st = carry
        c_l, n_l, m_l, g_c = inp
        m_new = jnp.maximum(g_c + m_st, m_l)
        a_prev = jnp.exp(g_c + m_st - m_new)
        a_loc = jnp.exp(m_l - m_new)
        c_new = a_prev[..., None, None] * c_st + a_loc[..., None, None] * c_l
        n_new = a_prev[..., None] * n_st + a_loc[..., None] * n_l
        return (c_new, n_new, m_new), (c_st, n_st, m_st)

    init = (jnp.zeros((bsz, nh, dk, dv), jnp.float32),
            jnp.zeros((bsz, nh, dk), jnp.float32),
            jnp.zeros((bsz, nh), jnp.float32))
    xs = (jnp.moveaxis(c_loc, 2, 0), jnp.moveaxis(n_loc, 2, 0),
          jnp.moveaxis(m_loc, 2, 0), jnp.moveaxis(g, 2, 0))
    _, (c0, n0, m0) = lax.scan(step, init, xs)
    c0 = jnp.moveaxis(c0, 0, 2)
    n0 = jnp.moveaxis(n0, 0, 2)
    m0 = jnp.moveaxis(m0, 0, 2)

    causal = jnp.tril(jnp.ones((CHUNK, CHUNK), dtype=bool))
    log_d = b[..., :, None] - b[..., None, :] + li[..., None, :]
    log_d = jnp.where(causal, log_d, -jnp.inf)
    log_inter = b + m0[..., None]
    m_t = jnp.maximum(log_inter, jnp.max(log_d, axis=-1))
    s_qk = jnp.einsum('bhctk,bhcsk->bhcts', q, k) * jnp.exp(log_d - m_t[..., None])
    e_inter = jnp.exp(log_inter - m_t)
    num = (jnp.einsum('bhcts,bhcsv->bhctv', s_qk, v)
           + e_inter[..., None] * jnp.einsum('bhctk,bhckv->bhctv', q, c0))
    den = jnp.sum(s_qk, axis=-1) + e_inter * jnp.einsum('bhctk,bhck->bhct', q, n0)
    h = num / jnp.maximum(jnp.abs(den), jnp.exp(-m_t))[..., None]
    return h.reshape(bsz, nh, seq, dv)


def mlstm_branch(q, k, v, o, ig, fg, b_ig, b_fg, norm_g):
    bsz, seq, _ = q.shape
    f32 = jnp.float32

    def heads(t, d):
        return t.astype(f32).reshape(bsz, seq, A_HEADS, d).transpose(0, 2, 1, 3)

    qh = heads(q, A_QK)
    kh = heads(k, A_QK) * (A_QK ** -0.5)
    vh = heads(v, A_V)
    li = (ig.astype(f32).reshape(bsz, seq, 2, A_HEADS) + b_ig.astype(f32)).transpose(0, 2, 3, 1)
    lf = jax.nn.log_sigmoid(fg.astype(f32).reshape(bsz, seq, 2, A_HEADS) + b_fg.astype(f32)).transpose(0, 2, 3, 1)
    flip = lambda t: jnp.flip(t, axis=2)
    h_fwd = mlstm_chunkwise(qh, kh, vh, li[:, 0], lf[:, 0])
    h_bwd = flip(mlstm_chunkwise(flip(qh), flip(kh), flip(vh), flip(li[:, 1]), flip(lf[:, 1])))
    h = (h_fwd + h_bwd).transpose(0, 2, 1, 3)
    h = h * lax.rsqrt(jnp.mean(h * h, axis=-1, keepdims=True) + NORM_EPS)
    h = h.reshape(bsz, seq, A_V_W) * norm_g.astype(f32)
    return (jax.nn.sigmoid(o.astype(f32)) * h).astype(q.dtype)


def centred_dwconv(x, w, b):
    pad = (CONV_K - 1) // 2
    y = lax.conv_general_dilated(x, w[:, None, :], window_strides=(1,), padding=[(pad, pad)],
                                 dimension_numbers=('NWC', 'WIO', 'NWC'),
                                 feature_group_count=x.shape[-1])
    return y + b


def ssd_chunked(x, dt, a, bm, cm):
    bsz, seq, nh, hp = x.shape
    ng, ns = bm.shape[2], bm.shape[3]
    ne = nh // ng
    nc = seq // CHUNK
    x = x.reshape(bsz, nc, CHUNK, ng, ne, hp)
    dt = dt.reshape(bsz, nc, CHUNK, ng, ne)
    bm = bm.reshape(bsz, nc, CHUNK, ng, ns)
    cm = cm.reshape(bsz, nc, CHUNK, ng, ns)
    acs = jnp.cumsum(dt * a.reshape(ng, ne), axis=2)
    xdt = x * dt[..., None]
    causal = jnp.tril(jnp.ones((CHUNK, CHUNK), dtype=bool))[:, :, None, None]
    seg = acs[:, :, :, None] - acs[:, :, None, :]
    decay = jnp.exp(jnp.where(causal, seg, -jnp.inf))
    cb = jnp.einsum('bctgn,bcsgn->bctsg', cm, bm)
    y_diag = jnp.einsum('bctsge,bcsgep->bctgep', cb[..., None] * decay, xdt)
    decay_to_end = jnp.exp(acs[:, :, -1:] - acs)
    states = jnp.einsum('bcsgn,bcsgep->bcgenp', bm, xdt * decay_to_end[..., None])
    chunk_decay = jnp.exp(acs[:, :, -1])

    def step(s, inp):
        st, dc = inp
        return dc[..., None, None] * s + st, s

    init = jnp.zeros((bsz, ng, ne, ns, hp), jnp.float32)
    _, s0 = lax.scan(step, init, (jnp.moveaxis(states, 1, 0), jnp.moveaxis(chunk_decay, 1, 0)))
    s0 = jnp.moveaxis(s0, 0, 1)
    y_off = jnp.einsum('bctgn,bcgenp->bctgep', cm, s0) * jnp.exp(acs)[..., None]
    return (y_diag + y_off).reshape(bsz, seq, nh, hp)


def mamba2_branch(z, xbc, dt_raw, conv_w, conv_b, a_log, dt_bias, d_skip, norm_g):
    bsz, seq, _ = z.shape
    f32 = jnp.float32
    xbc = jax.nn.silu(centred_dwconv(xbc, conv_w, conv_b)).astype(f32)
    xs, bm, cm = split_cols(xbc, (B_INNER, B_GROUPS * B_STATE, B_GROUPS * B_STATE))
    xh = xs.reshape(bsz, seq, B_HEADS, B_HEADDIM)
    bm = bm.reshape(bsz, seq, B_GROUPS, B_STATE)
    cm = cm.reshape(bsz, seq, B_GROUPS, B_STATE)
    dt = jax.nn.softplus(dt_raw.astype(f32).reshape(bsz, seq, 2, B_HEADS) + dt_bias.astype(f32))
    a = -jnp.exp(a_log.astype(f32))
    flip = lambda t: jnp.flip(t, axis=1)
    y_f = ssd_chunked(xh, dt[:, :, 0], a[0], bm, cm)
    y_b = flip(ssd_chunked(flip(xh), flip(dt[:, :, 1]), a[1], flip(bm), flip(cm)))
    y = y_f + y_b + d_skip.astype(f32)[:, None] * xh
    y = y.reshape(bsz, seq, B_INNER) * jax.nn.silu(z.astype(f32))
    return rmsnorm(y, norm_g).astype(z.dtype)


def apply_rope(t, positions):
    half = C_ROPE // 2
    inv_freq = ROPE_THETA ** (-jnp.arange(half, dtype=jnp.float32) / half)
    ang = positions.astype(jnp.float32)[:, :, None, None] * inv_freq
    cos, sin = jnp.cos(ang), jnp.sin(ang)
    tf = t.astype(jnp.float32)
    t1, t2 = tf[..., :half], tf[..., half:]
    return jnp.concatenate([t1 * cos - t2 * sin, t2 * cos + t1 * sin], axis=-1).astype(t.dtype)


def mla_branch(c_q, c_kv, k_rope, positions, q_norm, kv_norm, w_uq, w_ukv):
    bsz, seq, _ = c_q.shape
    dqk = C_NOPE + C_ROPE
    q = (rmsnorm(c_q, q_norm) @ w_uq).reshape(bsz, seq, C_HEADS, dqk)
    kv = (rmsnorm(c_kv, kv_norm) @ w_ukv).reshape(bsz, seq, C_HEADS, C_NOPE + C_V)
    q_nope, q_rot = q[..., :C_NOPE], q[..., C_NOPE:]
    k_nope, v = kv[..., :C_NOPE], kv[..., C_NOPE:]
    q_rot = apply_rope(q_rot, positions)
    k_rot = jnp.broadcast_to(apply_rope(k_rope[:, :, None, :], positions), (bsz, seq, C_HEADS, C_ROPE))
    q = jnp.concatenate([q_nope, q_rot], axis=-1) * (dqk ** -0.5)
    k = jnp.concatenate([k_nope, k_rot], axis=-1)
    nb = seq // CHUNK
    qb = q.reshape(bsz, nb, CHUNK, C_HEADS, dqk).transpose(1, 0, 2, 3, 4)

    def attend(q_blk):
        s = jnp.einsum('blhd,bshd->bhls', q_blk, k).astype(jnp.float32)
        pr = jax.nn.softmax(s, axis=-1).astype(v.dtype)
        return jnp.einsum('bhls,bshd->blhd', pr, v)

    o = lax.map(attend, qb)
    return o.transpose(1, 0, 2, 3, 4).reshape(bsz, seq, C_HEADS * C_V)


def setup_inputs(seed: int = 0) -> dict:
    key = jax.random.key(seed)
    keys = iter(jax.random.split(key, 48))
    f32 = jnp.float32
    L, D = DEPTH, D_MODEL

    def nrm(shape, scale):
        return jax.random.normal(next(keys), shape, f32) * scale

    def gain(shape):
        return 1.0 + 0.1 * jax.random.normal(next(keys), shape, f32)

    x = nrm((BATCH, SEQ, D), 1.0)
    p = nrm((DEPTH, BATCH, SEQ, PLE_DIM), 1.0)
    positions = (jnp.arange(SEQ, dtype=jnp.int32)[None, :]
                 + jax.random.randint(next(keys), (BATCH, 1), 0, MAX_POS_OFFSET, dtype=jnp.int32))
    ffn1_norm = gain((L, D))
    ffn1_w13 = nrm((L, D, 2 * D_FF), D ** -0.5)
    ffn1_w2 = nrm((L, D_FF, D), D_FF ** -0.5)
    mix_norm = gain((L, D))
    w_in = nrm((L, D, D_IN), D ** -0.5)
    mlstm_b_igate = nrm((L, 2, A_HEADS), 0.1)
    mlstm_b_fgate = jax.random.uniform(next(keys), (L, 2, A_HEADS), f32, 3.0, 6.0)
    mlstm_norm = gain((L, A_V_W))
    conv_w = nrm((L, CONV_K, B_XBC_W), CONV_K ** -0.5)
    conv_b = nrm((L, B_XBC_W), 0.01)
    ssm_a_log = jnp.log(jax.random.uniform(next(keys), (L, 2, B_HEADS), f32, 1.0, 16.0))
    dt0 = jnp.exp(jax.random.uniform(next(keys), (L, 2, B_HEADS), f32, math.log(DT_MIN), math.log(DT_MAX)))
    ssm_dt_bias = dt0 + jnp.log(-jnp.expm1(-dt0))
    ssm_d = gain((L, B_HEADS))
    ssm_norm = gain((L, B_INNER))
    mla_q_norm = gain((L, C_Q_RANK))
    mla_kv_norm = gain((L, C_KV_RANK))
    mla_w_uq = nrm((L, C_Q_RANK, C_HEADS * (C_NOPE + C_ROPE)), C_Q_RANK ** -0.5)
    mla_w_ukv = nrm((L, C_KV_RANK, C_HEADS * (C_NOPE + C_V)), C_KV_RANK ** -0.5)
    w_branch = nrm((L, N_BRANCH, BRANCH_W, D), BRANCH_W ** -0.5)
    w_out = nrm((L, D, D), D ** -0.5)
    ffn2_norm = gain((L, D))
    ffn2_w13 = nrm((L, D, 2 * D_FF), D ** -0.5)
    ffn2_w2 = nrm((L, D_FF, D), D_FF ** -0.5)
    ple_norm = gain((L, D))
    w_ple_gate = nrm((L, D, D), D ** -0.5)
    w_ple_proj = nrm((L, PLE_DIM, D), PLE_DIM ** -0.5)
    final_norm = gain((D,))
    return {"x": x, "p": p, "positions": positions,
            "ffn1_norm": ffn1_norm, "ffn1_w13": ffn1_w13, "ffn1_w2": ffn1_w2,
            "mix_norm": mix_norm, "w_in": w_in,
            "mlstm_b_igate": mlstm_b_igate, "mlstm_b_fgate": mlstm_b_fgate, "mlstm_norm": mlstm_norm,
            "conv_w": conv_w, "conv_b": conv_b, "ssm_a_log": ssm_a_log, "ssm_dt_bias": ssm_dt_bias,
            "ssm_d": ssm_d, "ssm_norm": ssm_norm,
            "mla_q_norm": mla_q_norm, "mla_kv_norm": mla_kv_norm, "mla_w_uq": mla_w_uq, "mla_w_ukv": mla_w_ukv,
            "w_branch": w_branch, "w_out": w_out,
            "ffn2_norm": ffn2_norm, "ffn2_w13": ffn2_w13, "ffn2_w2": ffn2_w2,
            "ple_norm": ple_norm, "w_ple_gate": w_ple_gate, "w_ple_proj": w_ple_proj,
            "final_norm": final_norm}


def reference(x, p, positions, ffn1_norm, ffn1_w13, ffn1_w2, mix_norm, w_in,
              mlstm_b_igate, mlstm_b_fgate, mlstm_norm, conv_w, conv_b,
              ssm_a_log, ssm_dt_bias, ssm_d, ssm_norm,
              mla_q_norm, mla_kv_norm, mla_w_uq, mla_w_ukv,
              w_branch, w_out, ffn2_norm, ffn2_w13, ffn2_w2,
              ple_norm, w_ple_gate, w_ple_proj, final_norm):
    bsz, seq, _ = x.shape
    h = x
    for i in range(DEPTH):
        h = h + HALF_STEP * swiglu(rmsnorm(h, ffn1_norm[i]), ffn1_w13[i], ffn1_w2[i])
        u = rmsnorm(h, mix_norm[i])
        (a_q, a_k, a_v, a_o, a_ig, a_fg, b_z, b_xbc, b_dt,
         c_q, c_kv, c_kr, gate_pre) = split_cols(u @ w_in[i], IN_SPLITS)
        y_a = mlstm_branch(a_q, a_k, a_v, a_o, a_ig, a_fg,
                           mlstm_b_igate[i], mlstm_b_fgate[i], mlstm_norm[i]) @ w_branch[i, 0]
        y_b = mamba2_branch(b_z, b_xbc, b_dt, conv_w[i], conv_b[i], ssm_a_log[i],
                            ssm_dt_bias[i], ssm_d[i], ssm_norm[i]) @ w_branch[i, 1]
        y_c = mla_branch(c_q, c_kv, c_kr, positions, mla_q_norm[i], mla_kv_norm[i],
                         mla_w_uq[i], mla_w_ukv[i]) @ w_branch[i, 2]
        gates = jax.nn.sigmoid(gate_pre.reshape(bsz, seq, N_BRANCH, D_MODEL))
        merged = gates[:, :, 0] * y_a + gates[:, :, 1] * y_b + gates[:, :, 2] * y_c
        h = h + merged @ w_out[i]
        h = h + HALF_STEP * swiglu(rmsnorm(h, ffn2_norm[i]), ffn2_w13[i], ffn2_w2[i])
        ple_gate = jax.nn.sigmoid(rmsnorm(h, ple_norm[i]) @ w_ple_gate[i])
        h = h + ple_gate * (p[i] @ w_ple_proj[i])
    return rmsnorm(h, final_norm)
```

```python
import functools
import math

import jax
import jax.numpy as jnp
from jax import lax
from jax.experimental import pallas as pl
from jax.experimental.pallas import tpu as pltpu

F32 = jnp.float32
BF16 = jnp.bfloat16

HALF_STEP = 0.5
NORM_EPS = 1e-6
CHUNK = 128
N_BRANCH = 3
A_HEADS = 4
A_QK = 128
A_V = 256
B_HEADS = 16
B_HEADDIM = 64
B_INNER = B_HEADS * B_HEADDIM
B_STATE = 128
B_GROUPS = 2
CONV_K = 5
C_HEADS = 8
C_Q_RANK = 512
C_KV_RANK = 512
C_NOPE = 128
C_ROPE = 64
C_V = 128
C_QK_PAD = 256
ROPE_THETA = 10000.0
BRANCH_W = 1024
A_QK_W = A_HEADS * A_QK
A_V_W = A_HEADS * A_V
B_BC_W = 2 * B_GROUPS * B_STATE
N_GATE_ROWS = 2 * A_HEADS + 2 * A_HEADS + 2 * B_HEADS
GATE_LANES = 128
DT_COL0 = 4 * A_HEADS
VMEM_LIMIT = 56 * 1024 * 1024
HI = lax.Precision.HIGHEST
NT_DIMS = (((1,), (1,)), ((), ()))
TN_DIMS = (((0,), (0,)), ((), ()))


def _cparams(*sem):
    return pltpu.CompilerParams(dimension_semantics=sem, vmem_limit_bytes=VMEM_LIMIT)


def _tile(n, pref):
    t = min(n, pref)
    while n % t:
        t //= 2
    return t


def _softplus(x):
    return jnp.maximum(x, 0.0) + jnp.log1p(jnp.exp(-jnp.abs(x)))


def _sigmoid(x):
    return 1.0 / (1.0 + jnp.exp(-x))


def _rms(x, g):
    return x * lax.rsqrt(jnp.mean(x * x, axis=-1, keepdims=True) + NORM_EPS) * g


def _layout(d_model):
    off = {}
    o = 0
    for name, w in (("a_v", A_V_W), ("a_o", A_V_W), ("b_z", B_INNER), ("b_x", B_INNER),
                    ("gate", N_BRANCH * d_model), ("a_q", A_QK_W), ("a_k", A_QK_W),
                    ("b_bc", B_BC_W), ("c_q", C_Q_RANK), ("c_kv", C_KV_RANK), ("c_kr", 2 * C_ROPE)):
        off[name] = o
        o += w
    return off, o


def _ffn_kernel(x_ref, g_ref, w1_ref, w3_ref, w2_ref, o_ref, xn_ref, acc_ref):
    f = pl.program_id(1)

    @pl.when(f == 0)
    def _():
        xn_ref[...] = _rms(x_ref[...], g_ref[...]).astype(BF16)
        acc_ref[...] = jnp.zeros_like(acc_ref)

    xn = xn_ref[...]
    a = jnp.dot(xn, w1_ref[...], preferred_element_type=F32)
    b = jnp.dot(xn, w3_ref[...], preferred_element_type=F32)
    hid = (a * _sigmoid(a) * b).astype(BF16)
    acc_ref[...] += jnp.dot(hid, w2_ref[...], preferred_element_type=F32)

    @pl.when(f == pl.num_programs(1) - 1)
    def _():
        o_ref[...] = x_ref[...] + HALF_STEP * acc_ref[...]


def _ffn(h, g, w13, w2):
    t, d = h.shape
    d_ff = w2.shape[0]
    tm = _tile(t, 512)
    tf = _tile(d_ff, 512)
    nf = d_ff // tf
    return pl.pallas_call(
        _ffn_kernel,
        out_shape=jax.ShapeDtypeStruct((t, d), F32),
        grid=(t // tm, nf),
        in_specs=[pl.BlockSpec((tm, d), lambda i, f: (i, 0)),
                  pl.BlockSpec((1, d), lambda i, f: (0, 0)),
                  pl.BlockSpec((d, tf), lambda i, f: (0, f)),
                  pl.BlockSpec((d, tf), lambda i, f: (0, nf + f)),
                  pl.BlockSpec((tf, d), lambda i, f: (f, 0))],
        out_specs=pl.BlockSpec((tm, d), lambda i, f: (i, 0)),
        scratch_shapes=[pltpu.VMEM((tm, d), BF16), pltpu.VMEM((tm, d), F32)],
        compiler_params=_cparams("parallel", "arbitrary"),
        name="ffn",
    )(h, g, w13, w13, w2)


def _gate_act(z, idx):
    return jnp.where(idx < 2 * A_HEADS, z,
                     jnp.where(idx < 4 * A_HEADS, -_softplus(-z), _softplus(z)))


def _proj_kernel(x_ref, g_ref, w_ref, wg_ref, wgt_ref, bc_ref, br_ref, o_ref, gc_ref, gr_ref, xn_ref):
    @pl.when(pl.program_id(1) == 0)
    def _():
        xn = _rms(x_ref[...], g_ref[...]).astype(BF16)
        xn_ref[...] = xn
        zc = jnp.dot(xn, wg_ref[...], preferred_element_type=F32) + bc_ref[...]
        gc_ref[...] = _gate_act(zc, lax.broadcasted_iota(jnp.int32, zc.shape, 1))
        zr = lax.dot_general(wgt_ref[...], xn, NT_DIMS, preferred_element_type=F32) + br_ref[...]
        gr_ref[...] = _gate_act(zr, lax.broadcasted_iota(jnp.int32, zr.shape, 0))

    o_ref[...] = jnp.dot(xn_ref[...], w_ref[...], preferred_element_type=F32).astype(o_ref.dtype)


def _proj(h, g, w_main, wg, wgt, bias_c, bias_r):
    t, d = h.shape
    n = w_main.shape[1]
    tm = _tile(t, 1024)
    tn = _tile(n, 1024)
    return pl.pallas_call(
        _proj_kernel,
        out_shape=(jax.ShapeDtypeStruct((t, n), BF16),
                   jax.ShapeDtypeStruct((t, GATE_LANES), F32),
                   jax.ShapeDtypeStruct((N_GATE_ROWS, t), F32)),
        grid=(t // tm, n // tn),
        in_specs=[pl.BlockSpec((tm, d), lambda i, j: (i, 0)),
                  pl.BlockSpec((1, d), lambda i, j: (0, 0)),
                  pl.BlockSpec((d, tn), lambda i, j: (0, j)),
                  pl.BlockSpec((d, GATE_LANES), lambda i, j: (0, 0)),
                  pl.BlockSpec((N_GATE_ROWS, d), lambda i, j: (0, 0)),
                  pl.BlockSpec((1, GATE_LANES), lambda i, j: (0, 0)),
                  pl.BlockSpec((N_GATE_ROWS, 1), lambda i, j: (0, 0))],
        out_specs=(pl.BlockSpec((tm, tn), lambda i, j: (i, j)),
                   pl.BlockSpec((tm, GATE_LANES), lambda i, j: (i, 0)),
                   pl.BlockSpec((N_GATE_ROWS, tm), lambda i, j: (0, i))),
        scratch_shapes=[pltpu.VMEM((tm, d), BF16)],
        compiler_params=_cparams("parallel", "arbitrary"),
        name="in_proj",
    )(h, g, w_main, wg, wgt, bias_c, bias_r)


def _chunk_mask(reverse):
    ti = lax.broadcasted_iota(jnp.int32, (CHUNK, CHUNK), 0)
    si = lax.broadcasted_iota(jnp.int32, (CHUNK, CHUNK), 1)
    return (si >= ti) if reverse else (si <= ti)


def _mlstm_kernel(*refs, reverse, final):
    if final:
        (q_ref, k_ref, v_ref, gc_ref, gr_ref, o_ref, hf_ref, ng_ref,
         out_ref, c_ref, n_ref, m_ref) = refs
    else:
        q_ref, k_ref, v_ref, gc_ref, gr_ref, out_ref, c_ref, n_ref, m_ref = refs

    @pl.when(pl.program_id(1) == 0)
    def _():
        c_ref[...] = jnp.zeros_like(c_ref)
        n_ref[...] = jnp.zeros_like(n_ref)
        m_ref[...] = jnp.zeros_like(m_ref)

    allowed = _chunk_mask(reverse)
    amat = allowed.astype(F32)
    gc = gc_ref[0]
    gr = gr_ref[...]
    cum_c = jnp.dot(amat, gc, precision=HI, preferred_element_type=F32)
    cum_r = lax.dot_general(gr, amat, NT_DIMS, precision=HI, preferred_element_type=F32)
    scale = A_QK ** -0.5
    d = 1 if reverse else 0
    for hd in range(A_HEADS):
        ii = d * A_HEADS + hd
        fi = 2 * A_HEADS + d * A_HEADS + hd
        li_c = gc[:, ii:ii + 1]
        li_r = gr[ii:ii + 1, :]
        b_c = cum_c[:, fi:fi + 1]
        b_r = cum_r[fi:fi + 1, :]
        g_tot = jnp.sum(gc[:, fi:fi + 1], axis=0, keepdims=True)
        m0 = m_ref[hd][:, 0:1]
        c0 = c_ref[hd]
        n0 = n_ref[hd]
        qh = q_ref[0, :, hd * A_QK:(hd + 1) * A_QK]
        kh = k_ref[0, :, hd * A_QK:(hd + 1) * A_QK]
        vh = v_ref[0, :, hd * A_V:(hd + 1) * A_V]
        kf = kh.astype(F32)
        w_state = g_tot - b_c + li_c
        m_loc = jnp.max(w_state, axis=0, keepdims=True)
        es = jnp.exp(w_state - m_loc) * scale
        ke = kf * es
        c_loc = lax.dot_general(ke.astype(BF16), vh, TN_DIMS, preferred_element_type=F32)
        n_loc = jnp.sum(ke, axis=0, keepdims=True)
        log_d = jnp.where(allowed, b_c - b_r + li_r, -jnp.inf)
        log_inter = b_c + m0
        m_t = jnp.maximum(log_inter, jnp.max(log_d, axis=1, keepdims=True))
        qk = lax.dot_general(qh, kh, NT_DIMS, preferred_element_type=F32)
        s_qk = qk * (jnp.exp(log_d - m_t) * scale)
        e_inter = jnp.exp(log_inter - m_t)
        num = (jnp.dot(s_qk.astype(BF16), vh, preferred_element_type=F32)
               + e_inter * jnp.dot(qh, c0.astype(BF16), preferred_element_type=F32))
        den = (jnp.sum(s_qk, axis=1, keepdims=True)
               + e_inter * jnp.sum(qh.astype(F32) * n0, axis=1, keepdims=True))
        hout = num / jnp.maximum(jnp.abs(den), jnp.exp(-m_t))
        m_new = jnp.maximum(g_tot + m0, m_loc)
        a_prev = jnp.exp(g_tot + m0 - m_new)
        a_loc = jnp.exp(m_loc - m_new)
        c_ref[hd] = a_prev * c0 + a_loc * c_loc
        n_ref[hd] = a_prev * n0 + a_loc * n_loc
        m_ref[hd] = jnp.broadcast_to(m_new, (1, GATE_LANES))
        sl = slice(hd * A_V, (hd + 1) * A_V)
        if final:
            htot = hout + hf_ref[0, :, sl]
            hn = htot * lax.rsqrt(jnp.mean(htot * htot, axis=-1, keepdims=True) + NORM_EPS)
            out_ref[0, :, sl] = (_sigmoid(o_ref[0, :, sl].astype(F32)) * hn * ng_ref[:, sl]).astype(out_ref.dtype)
        else:
            out_ref[0, :, sl] = hout


def _mlstm_dir(main3, gc3, gr, off, reverse, h_fwd=None, norm_g=None):
    bsz, seq, _ = main3.shape
    nc = seq // CHUNK
    final = h_fwd is not None

    def cidx(c):
        return (nc - 1 - c) if reverse else c

    def col(name, width):
        blk = off[name] // width
        return pl.BlockSpec((1, CHUNK, width), lambda b, c: (b, cidx(c), blk))

    in_specs = [col("a_q", A_QK_W), col("a_k", A_QK_W), col("a_v", A_V_W),
                pl.BlockSpec((1, CHUNK, GATE_LANES), lambda b, c: (b, cidx(c), 0)),
                pl.BlockSpec((N_GATE_ROWS, CHUNK), lambda b, c: (0, b * nc + cidx(c)))]
    args = [main3, main3, main3, gc3, gr]
    if final:
        in_specs += [col("a_o", A_V_W),
                     pl.BlockSpec((1, CHUNK, A_V_W), lambda b, c: (b, cidx(c), 0)),
                     pl.BlockSpec((1, A_V_W), lambda b, c: (0, 0))]
        args += [main3, h_fwd, norm_g]
    return pl.pallas_call(
        functools.partial(_mlstm_kernel, reverse=reverse, final=final),
        out_shape=jax.ShapeDtypeStruct((bsz, seq, A_V_W), BF16 if final else F32),
        grid=(bsz, nc),
        in_specs=in_specs,
        out_specs=pl.BlockSpec((1, CHUNK, A_V_W), lambda b, c: (b, cidx(c), 0)),
        scratch_shapes=[pltpu.VMEM((A_HEADS, A_QK, A_V), F32),
                        pltpu.VMEM((A_HEADS, 1, A_QK), F32),
                        pltpu.VMEM((A_HEADS, 1, GATE_LANES), F32)],
        compiler_params=_cparams("parallel", "arbitrary"),
        name="mlstm_bwd" if reverse else "mlstm_fwd",
    )(*args)


def _conv_kernel(xc_ref, xp_ref, xn_ref, bc_ref, bp_ref, bn_ref, w_ref, b_ref, ox_ref, obc_ref, sx_ref, sbc_ref):
    i = pl.program_id(1)
    first = i == 0
    last = i == pl.num_programs(1) - 1
    pad = (CONV_K - 1) // 2

    def run(cur_ref, prev_ref, next_ref, scr_ref, out_ref, c0):
        tb, w = cur_ref.shape[1], cur_ref.shape[2]
        scr_ref[0:8, :] = jnp.where(first, 0.0, prev_ref[0].astype(F32))
        scr_ref[8:8 + tb, :] = cur_ref[0].astype(F32)
        scr_ref[8 + tb:16 + tb, :] = jnp.where(last, 0.0, next_ref[0].astype(F32))
        acc = jnp.zeros((tb, w), F32) + b_ref[:, c0:c0 + w]
        for j in range(CONV_K):
            acc = acc + scr_ref[8 - pad + j:8 - pad + j + tb, :] * w_ref[j:j + 1, c0:c0 + w]
        out_ref[0] = (acc * _sigmoid(acc)).astype(out_ref.dtype)

    run(xc_ref, xp_ref, xn_ref, sx_ref, ox_ref, 0)
    run(bc_ref, bp_ref, bn_ref, sbc_ref, obc_ref, B_INNER)


def _conv(main3, conv_w, conv_b, off):
    bsz, seq, _ = main3.shape
    tb = _tile(seq, 512)
    nb = seq // tb
    r8 = tb // 8
    last8 = seq // 8 - 1

    def trio(name, width):
        blk = off[name] // width
        return [pl.BlockSpec((1, tb, width), lambda b, i: (b, i, blk)),
                pl.BlockSpec((1, 8, width), lambda b, i: (b, jnp.maximum(i * r8 - 1, 0), blk)),
                pl.BlockSpec((1, 8, width), lambda b, i: (b, jnp.minimum((i + 1) * r8, last8), blk))]

    wtot = B_INNER + B_BC_W
    return pl.pallas_call(
        _conv_kernel,
        out_shape=(jax.ShapeDtypeStruct((bsz, seq, B_INNER), BF16),
                   jax.ShapeDtypeStruct((bsz, seq, B_BC_W), BF16)),
        grid=(bsz, nb),
        in_specs=trio("b_x", B_INNER) + trio("b_bc", B_BC_W) + [
            pl.BlockSpec((CONV_K, wtot), lambda b, i: (0, 0)),
            pl.BlockSpec((1, wtot), lambda b, i: (0, 0))],
        out_specs=(pl.BlockSpec((1, tb, B_INNER), lambda b, i: (b, i, 0)),
                   pl.BlockSpec((1, tb, B_BC_W), lambda b, i: (b, i, 0))),
        scratch_shapes=[pltpu.VMEM((tb + 16, B_INNER), F32), pltpu.VMEM((tb + 16, B_BC_W), F32)],
        compiler_params=_cparams("parallel", "parallel"),
        name="dwconv_silu",
    )(main3, main3, main3, main3, main3, main3, conv_w, conv_b)


def _ssd_kernel(*refs, reverse, final):
    if final:
        (xs_ref, bc_ref, gc_ref, gr_ref, alr_ref, alc_ref, z_ref, yf_ref, dsk_ref, ng_ref,
         out_ref, s_ref, y_ref) = refs
    else:
        xs_ref, bc_ref, gc_ref, gr_ref, alr_ref, alc_ref, out_ref, s_ref, y_ref = refs

    @pl.when(pl.program_id(1) == 0)
    def _():
        s_ref[...] = jnp.zeros_like(s_ref)

    d = 1 if reverse else 0
    col0 = DT_COL0 + d * B_HEADS
    allowed = _chunk_mask(reverse)
    amat = allowed.astype(F32)
    gc = gc_ref[0]
    gr = gr_ref[...]
    lane = lax.broadcasted_iota(jnp.int32, (1, GATE_LANES), 1)
    a_row = jnp.where((lane >= col0) & (lane < col0 + B_HEADS), -jnp.exp(alr_ref[...]), 0.0)
    row = lax.broadcasted_iota(jnp.int32, (N_GATE_ROWS, 1), 0)
    a_col = jnp.where((row >= col0) & (row < col0 + B_HEADS), -jnp.exp(alc_ref[...]), 0.0)
    dta_c = gc * a_row
    dta_r = gr * a_col
    acs_c = jnp.dot(amat, dta_c, precision=HI, preferred_element_type=F32)
    acs_r = lax.dot_general(dta_r, amat, NT_DIMS, precision=HI, preferred_element_type=F32)
    tot = jnp.sum(dta_c, axis=0, keepdims=True)
    er = lax.broadcasted_iota(jnp.int32, (GATE_LANES, B_INNER), 0)
    ec = lax.broadcasted_iota(jnp.int32, (GATE_LANES, B_INNER), 1)
    expand = (er == col0 + (ec >> (B_HEADDIM.bit_length() - 1))).astype(F32)

    def spread(t):
        return jnp.dot(t, expand, precision=HI, preferred_element_type=F32)

    dt_x = spread(gc)
    to_end_x = spread(jnp.exp(tot - acs_c))
    from_start_x = spread(jnp.exp(acs_c))
    chunk_decay_x = spread(jnp.broadcast_to(jnp.exp(tot), (8, GATE_LANES)))[0:1]

    xs = xs_ref[0].astype(F32)
    xdt = xs * dt_x
    xdt_b = xdt.astype(BF16)
    xend_b = (xdt * to_end_x).astype(BF16)
    gw = B_INNER // B_GROUPS
    hpg = B_HEADS // B_GROUPS
    for g in range(B_GROUPS):
        bm = bc_ref[0, :, g * B_STATE:(g + 1) * B_STATE]
        cm = bc_ref[0, :, (B_GROUPS + g) * B_STATE:(B_GROUPS + g + 1) * B_STATE]
        gs = slice(g * gw, (g + 1) * gw)
        s0 = s_ref[g]
        y_ref[:, gs] = jnp.dot(cm, s0.astype(BF16), preferred_element_type=F32) * from_start_x[:, gs]
        states = lax.dot_general(bm, xend_b[:, gs], TN_DIMS, preferred_element_type=F32)
        s_ref[g] = chunk_decay_x[:, gs] * s0 + states
        cb = lax.dot_general(cm, bm, NT_DIMS, preferred_element_type=F32)
        for hh in range(hpg):
            ci = col0 + g * hpg + hh
            seg = acs_c[:, ci:ci + 1] - acs_r[ci:ci + 1, :]
            dec = jnp.exp(jnp.where(allowed, seg, -jnp.inf))
            hs = slice((g * hpg + hh) * B_HEADDIM, (g * hpg + hh + 1) * B_HEADDIM)
            y_ref[:, hs] += jnp.dot((cb * dec).astype(BF16), xdt_b[:, hs], preferred_element_type=F32)
    if final:
        y = y_ref[...] + yf_ref[0] + dsk_ref[...] * xs
        zf = z_ref[0].astype(F32)
        out_ref[0] = _rms(y * (zf * _sigmoid(zf)), ng_ref[...]).astype(out_ref.dtype)
    else:
        out_ref[0] = y_ref[...]


def _ssd_dir(xs3, bc3, gc3, gr, alog_r, alog_c, reverse, main3=None, off=None, y_fwd=None, d_skip=None, norm_g=None):
    bsz, seq, _ = xs3.shape
    nc = seq // CHUNK
    final = y_fwd is not None

    def cidx(c):
        return (nc - 1 - c) if reverse else c

    def seq_spec(width, blk=0):
        return pl.BlockSpec((1, CHUNK, width), lambda b, c: (b, cidx(c), blk))

    def const_spec(shape):
        return pl.BlockSpec(shape, lambda b, c: (0, 0))

    in_specs = [seq_spec(B_INNER), seq_spec(B_BC_W), seq_spec(GATE_LANES),
                pl.BlockSpec((N_GATE_ROWS, CHUNK), lambda b, c: (0, b * nc + cidx(c))),
                const_spec((1, GATE_LANES)), const_spec((N_GATE_ROWS, 1))]
    args = [xs3, bc3, gc3, gr, alog_r, alog_c]
    if final:
        in_specs += [seq_spec(B_INNER, off["b_z"] // B_INNER), seq_spec(B_INNER),
                     const_spec((1, B_INNER)), const_spec((1, B_INNER))]
        args += [main3, y_fwd, d_skip, norm_g]
    return pl.pallas_call(
        functools.partial(_ssd_kernel, reverse=reverse, final=final),
        out_shape=jax.ShapeDtypeStruct((bsz, seq, B_INNER), BF16 if final else F32),
        grid=(bsz, nc),
        in_specs=in_specs,
        out_specs=seq_spec(B_INNER),
        scratch_shapes=[pltpu.VMEM((B_GROUPS, B_STATE, B_INNER // B_GROUPS), F32),
                        pltpu.VMEM((CHUNK, B_INNER), F32)],
        compiler_params=_cparams("parallel", "arbitrary"),
        name="ssd_bwd" if reverse else "ssd_fwd",
    )(*args)


def _rope_kernel(pos_ref, tab_ref):
    half = C_ROPE // 2
    lane = lax.broadcasted_iota(jnp.int32, (1, 4 * half), 1)
    fr = (lane & (half - 1)).astype(F32)
    inv_freq = jnp.exp(fr * (-math.log(ROPE_THETA) / half))
    ang = pos_ref[...].astype(F32) * inv_freq
    sn = jnp.sin(ang)
    tab_ref[...] = jnp.where(lane < 2 * half, jnp.cos(ang), jnp.where(lane < 3 * half, -sn, sn))


def _rope_table(pos_col):
    t = pos_col.shape[0]
    tm = _tile(t, 1024)
    return pl.pallas_call(
        _rope_kernel,
        out_shape=jax.ShapeDtypeStruct((t, 2 * C_ROPE), F32),
        grid=(t // tm,),
        in_specs=[pl.BlockSpec((tm, 1), lambda i: (i, 0))],
        out_specs=pl.BlockSpec((tm, 2 * C_ROPE), lambda i: (i, 0)),
        compiler_params=_cparams("parallel"),
        name="rope_table",
    )(pos_col)


def _mla_prep_kernel(cq_ref, ckv_ref, ckr_ref, tab_ref, qg_ref, kvg_ref, wq_ref, wkv_ref, q_ref, k_ref, v_ref):
    tab = tab_ref[0]
    lane = lax.broadcasted_iota(jnp.int32, tab.shape, 1)

    def rope(t, zero_upper):
        p = t * tab
        r = p + pltpu.roll(p, C_ROPE, axis=1)
        return jnp.where(lane < C_ROPE, r, 0.0) if zero_upper else r

    cqn = _rms(cq_ref[0].astype(F32), qg_ref[...]).astype(BF16)
    ckvn = _rms(ckv_ref[0].astype(F32), kvg_ref[...]).astype(BF16)
    q_all = jnp.dot(cqn, wq_ref[...], preferred_element_type=F32)
    kv_all = jnp.dot(ckvn, wkv_ref[...], preferred_element_type=F32)
    k_rot = rope(ckr_ref[0].astype(F32), False).astype(BF16)
    qscale = (C_NOPE + C_ROPE) ** -0.5
    for hd in range(C_HEADS):
        q0 = hd * C_QK_PAD
        q_ref[0, hd, :, 0:C_NOPE] = (q_all[:, q0:q0 + C_NOPE] * qscale).astype(BF16)
        q_ref[0, hd, :, C_NOPE:C_QK_PAD] = (rope(q_all[:, q0 + C_NOPE:q0 + C_QK_PAD], True) * qscale).astype(BF16)
        k_ref[0, hd, :, 0:C_NOPE] = kv_all[:, hd * C_NOPE:(hd + 1) * C_NOPE].astype(BF16)
        k_ref[0, hd, :, C_NOPE:C_QK_PAD] = k_rot
        v0 = C_HEADS * C_NOPE + hd * C_V
        v_ref[0, hd] = kv_all[:, v0:v0 + C_V].astype(BF16)


def _mla_prep(main3, tab3, q_g, kv_g, wq, wkv, off):
    bsz, seq, _ = main3.shape
    tm = _tile(seq, 512)

    def col(name, width):
        blk = off[name] // width
        return pl.BlockSpec((1, tm, width), lambda b, i: (b, i, blk))

    def const_spec(shape):
        return pl.BlockSpec(shape, lambda b, i: (0, 0))

    def head_spec(width):
        return pl.BlockSpec((1, C_HEADS, tm, width), lambda b, i: (b, 0, i, 0))

    return pl.pallas_call(
        _mla_prep_kernel,
        out_shape=(jax.ShapeDtypeStruct((bsz, C_HEADS, seq, C_QK_PAD), BF16),
                   jax.ShapeDtypeStruct((bsz, C_HEADS, seq, C_QK_PAD), BF16),
                   jax.ShapeDtypeStruct((bsz, C_HEADS, seq, C_V), BF16)),
        grid=(bsz, seq // tm),
        in_specs=[col("c_q", C_Q_RANK), col("c_kv", C_KV_RANK), col("c_kr", 2 * C_ROPE),
                  pl.BlockSpec((1, tm, 2 * C_ROPE), lambda b, i: (b, i, 0)),
                  const_spec((1, C_Q_RANK)), const_spec((1, C_KV_RANK)),
                  const_spec(wq.shape), const_spec(wkv.shape)],
        out_specs=(head_spec(C_QK_PAD), head_spec(C_QK_PAD), head_spec(C_V)),
        compiler_params=_cparams("parallel", "parallel"),
        name="mla_prep",
    )(main3, main3, main3, tab3, q_g, kv_g, wq, wkv)


def _attn_kernel(q_ref, k_ref, v_ref, o_ref, *, tk):
    q = q_ref[0, 0]
    tq = q.shape[0]
    nk = k_ref.shape[2] // tk

    def body(j, carry):
        m, l, acc = carry
        ks = pl.multiple_of(j * tk, tk)
        kb = k_ref[0, 0, pl.ds(ks, tk), :]
        vb = v_ref[0, 0, pl.ds(ks, tk), :]
        s = lax.dot_general(q, kb, NT_DIMS, preferred_element_type=F32)
        m_new = jnp.maximum(m, jnp.max(s, axis=1, keepdims=True))
        alpha = jnp.exp(m - m_new)
        p = jnp.exp(s - m_new)
        l = alpha * l + jnp.sum(p, axis=1, keepdims=True)
        acc = alpha * acc + jnp.dot(p.astype(BF16), vb, preferred_element_type=F32)
        return m_new, l, acc

    init = (jnp.full((tq, 1), -jnp.inf, F32), jnp.zeros((tq, 1), F32), jnp.zeros((tq, C_V), F32))
    _, l, acc = lax.fori_loop(0, nk, body, init)
    o_ref[0] = (acc / l).astype(o_ref.dtype)


def _attention(q4, k4, v4):
    bsz, nh, seq, _ = q4.shape
    tq = _tile(seq, 512)
    tk = _tile(seq, 512)
    return pl.pallas_call(
        functools.partial(_attn_kernel, tk=tk),
        out_shape=jax.ShapeDtypeStruct((bsz, seq, nh * C_V), BF16),
        grid=(bsz, nh, seq // tq),
        in_specs=[pl.BlockSpec((1, 1, tq, C_QK_PAD), lambda b, h, i: (b, h, i, 0)),
                  pl.BlockSpec((1, 1, seq, C_QK_PAD), lambda b, h, i: (b, h, 0, 0)),
                  pl.BlockSpec((1, 1, seq, C_V), lambda b, h, i: (b, h, 0, 0))],
        out_specs=pl.BlockSpec((1, tq, C_V), lambda b, h, i: (b, i, h)),
        compiler_params=_cparams("parallel", "parallel", "parallel"),
        name="mla_attention",
    )(q4, k4, v4)


def _merge_kernel(ya_ref, yb_ref, yc_ref, wb_ref, g0_ref, g1_ref, g2_ref, o_ref):
    acc = None
    for y_ref, g_ref, k in ((ya_ref, g0_ref, 0), (yb_ref, g1_ref, 1), (yc_ref, g2_ref, 2)):
        t = _sigmoid(g_ref[...].astype(F32)) * jnp.dot(y_ref[...], wb_ref[k], preferred_element_type=F32)
        acc = t if acc is None else acc + t
    o_ref[...] = acc.astype(o_ref.dtype)


def _merge(ya, yb, yc, wb, main, off, d_model):
    t = ya.shape[0]
    tm = _tile(t, 1024)
    tn = _tile(d_model, 512)
    gblk = off["gate"] // tn
    per = d_model // tn

    def gate_spec(k):
        return pl.BlockSpec((tm, tn), lambda i, j: (i, gblk + k * per + j))

    y_spec = pl.BlockSpec((tm, BRANCH_W), lambda i, j: (i, 0))
    return pl.pallas_call(
        _merge_kernel,
        out_shape=jax.ShapeDtypeStruct((t, d_model), BF16),
        grid=(t // tm, per),
        in_specs=[y_spec, y_spec, y_spec,
                  pl.BlockSpec((N_BRANCH, BRANCH_W, tn), lambda i, j: (0, 0, j)),
                  gate_spec(0), gate_spec(1), gate_spec(2)],
        out_specs=pl.BlockSpec((tm, tn), lambda i, j: (i, j)),
        compiler_params=_cparams("parallel", "parallel"),
        name="branch_merge",
    )(ya, yb, yc, wb, main, main, main)


def _mm_res_kernel(a_ref, w_ref, r_ref, o_ref):
    o_ref[...] = r_ref[...] + jnp.dot(a_ref[...], w_ref[...], preferred_element_type=F32)


def _mm_res(a, w, res):
    t, k = a.shape
    n = w.shape[1]
    tm = _tile(t, 1024)
    tn = _tile(n, 1024)
    return pl.pallas_call(
        _mm_res_kernel,
        out_shape=jax.ShapeDtypeStruct((t, n), F32),
        grid=(t // tm, n // tn),
        in_specs=[pl.BlockSpec((tm, k), lambda i, j: (i, 0)),
                  pl.BlockSpec((k, tn), lambda i, j: (0, j)),
                  pl.BlockSpec((tm, tn), lambda i, j: (i, j))],
        out_specs=pl.BlockSpec((tm, tn), lambda i, j: (i, j)),
        compiler_params=_cparams("parallel", "parallel"),
        name="out_proj",
    )(a, w, res)


def _ple_kernel(x_ref, g_ref, wg_ref, p_ref, wp_ref, r_ref, o_ref, xn_ref):
    @pl.when(pl.program_id(1) == 0)
    def _():
        xn_ref[...] = _rms(x_ref[...], g_ref[...]).astype(BF16)

    gate = _sigmoid(jnp.dot(xn_ref[...], wg_ref[...], preferred_element_type=F32))
    emb = jnp.dot(p_ref[...].astype(BF16), wp_ref[...], preferred_element_type=F32)
    o_ref[...] = r_ref[...] + gate * emb


def _ple(h, g, wgate, p, wproj):
    t, d = h.shape
    pd = p.shape[1]
    tm = _tile(t, 1024)
    tn = _tile(d, 1024)
    return pl.pallas_call(
        _ple_kernel,
        out_shape=jax.ShapeDtypeStruct((t, d), F32),
        grid=(t // tm, d // tn),
        in_specs=[pl.BlockSpec((tm, d), lambda i, j: (i, 0)),
                  pl.BlockSpec((1, d), lambda i, j: (0, 0)),
                  pl.BlockSpec((d, tn), lambda i, j: (0, j)),
                  pl.BlockSpec((tm, pd), lambda i, j: (i, 0)),
                  pl.BlockSpec((pd, tn), lambda i, j: (0, j)),
                  pl.BlockSpec((tm, tn), lambda i, j: (i, j))],
        out_specs=pl.BlockSpec((tm, tn), lambda i, j: (i, j)),
        scratch_shapes=[pltpu.VMEM((tm, d), BF16)],
        compiler_params=_cparams("parallel", "arbitrary"),
        name="ple",
    )(h, g, wgate, p, wproj, h)


def _norm_kernel(x_ref, g_ref, o_ref):
    o_ref[...] = _rms(x_ref[...], g_ref[...])


def _final_norm(h, g):
    t, d = h.shape
    tm = _tile(t, 512)
    return pl.pallas_call(
        _norm_kernel,
        out_shape=jax.ShapeDtypeStruct((t, d), F32),
        grid=(t // tm,),
        in_specs=[pl.BlockSpec((tm, d), lambda i: (i, 0)), pl.BlockSpec((1, d), lambda i: (0, 0))],
        out_specs=pl.BlockSpec((tm, d), lambda i: (i, 0)),
        compiler_params=_cparams("parallel"),
        name="final_norm",
    )(h, g)


def _prep_w_in(w_in, d_model):
    sizes = (A_QK_W, A_QK_W, A_V_W, A_V_W, 2 * A_HEADS, 2 * A_HEADS, B_INNER, B_INNER + B_BC_W,
             2 * B_HEADS, C_Q_RANK, C_KV_RANK, C_ROPE, N_BRANCH * d_model)
    cuts, o = [], 0
    for s in sizes:
        cuts.append((o, o + s))
        o += s
    (a_q, a_k, a_v, a_o, a_ig, a_fg, b_z, b_xbc, b_dt, c_q, c_kv, c_kr, gate) = [w_in[..., lo:hi] for lo, hi in cuts]
    half = C_ROPE // 2
    kr_swapped = jnp.concatenate([c_kr[..., half:], c_kr[..., :half]], axis=-1)
    pieces = {"a_v": a_v, "a_o": a_o, "b_z": b_z, "b_x": b_xbc[..., :B_INNER], "gate": gate, "a_q": a_q,
              "a_k": a_k, "b_bc": b_xbc[..., B_INNER:], "c_q": c_q, "c_kv": c_kv,
              "c_kr": jnp.concatenate([c_kr, kr_swapped], axis=-1)}
    off, n1 = _layout(d_model)
    order = sorted(off, key=off.get)
    n_pad = -n1 % 1024
    cols = [pieces[k] for k in order]
    if n_pad:
        cols.append(jnp.zeros(w_in.shape[:-1] + (n_pad,), w_in.dtype))
    w_main = jnp.concatenate(cols, axis=-1).astype(BF16)
    gates = jnp.concatenate([a_ig, a_fg, b_dt], axis=-1)
    wg = jnp.concatenate([gates, jnp.zeros(w_in.shape[:-1] + (GATE_LANES - N_GATE_ROWS,), w_in.dtype)],
                         axis=-1).astype(BF16)
    wgt = jnp.swapaxes(gates, -1, -2).astype(BF16)
    return w_main, wg, wgt, off


def _prep_mla(w_uq, w_ukv):
    nl, rq, _ = w_uq.shape
    half = C_ROPE // 2
    wq = w_uq.reshape(nl, rq, C_HEADS, C_NOPE + C_ROPE)
    t1 = wq[..., C_NOPE:C_NOPE + half]
    t2 = wq[..., C_NOPE + half:]
    wq = jnp.concatenate([wq[..., :C_NOPE], t1, t2, t2, t1], axis=-1).reshape(nl, rq, C_HEADS * C_QK_PAD)
    wkv = w_ukv.reshape(nl, w_ukv.shape[1], C_HEADS, C_NOPE + C_V)
    wkv = jnp.concatenate([wkv[..., :C_NOPE].reshape(nl, -1, C_HEADS * C_NOPE),
                           wkv[..., C_NOPE:].reshape(nl, -1, C_HEADS * C_V)], axis=-1)
    return wq.astype(BF16), wkv.astype(BF16)


def kernel(x, p, positions, ffn1_norm, ffn1_w13, ffn1_w2, mix_norm, w_in, mlstm_b_igate, mlstm_b_fgate, mlstm_norm, conv_w, conv_b, ssm_a_log, ssm_dt_bias, ssm_d, ssm_norm, mla_q_norm, mla_kv_norm, mla_w_uq, mla_w_ukv, w_branch, w_out, ffn2_norm, ffn2_w13, ffn2_w2, ple_norm, w_ple_gate, w_ple_proj, final_norm):
    bsz, seq, d = x.shape
    depth = w_in.shape[0]
    t = bsz * seq
    nl = depth

    w_main, wg, wgt, off = _prep_w_in(w_in, d)
    wq, wkv = _prep_mla(mla_w_uq, mla_w_ukv)
    ffn1_w13b, ffn1_w2b = ffn1_w13.astype(BF16), ffn1_w2.astype(BF16)
    ffn2_w13b, ffn2_w2b = ffn2_w13.astype(BF16), ffn2_w2.astype(BF16)
    w_branch_b, w_out_b = w_branch.astype(BF16), w_out.astype(BF16)
    w_pg_b, w_pp_b = w_ple_gate.astype(BF16), w_ple_proj.astype(BF16)
    gate_bias = jnp.concatenate([mlstm_b_igate.reshape(nl, -1), mlstm_b_fgate.reshape(nl, -1),
                                 ssm_dt_bias.reshape(nl, -1)], axis=-1).astype(F32)
    bias_c = jnp.pad(gate_bias, ((0, 0), (0, GATE_LANES - N_GATE_ROWS)))[:, None, :]
    bias_r = gate_bias[:, :, None]
    alog = ssm_a_log.reshape(nl, -1).astype(F32)
    alog_r = jnp.pad(alog, ((0, 0), (DT_COL0, GATE_LANES - N_GATE_ROWS)))[:, None, :]
    alog_c = jnp.pad(alog, ((0, 0), (DT_COL0, 0)))[:, :, None]
    d_skip = jnp.repeat(ssm_d.astype(F32), B_HEADDIM, axis=-1)[:, None, :]

    tab3 = _rope_table(positions.reshape(t, 1)).reshape(bsz, seq, 2 * C_ROPE)

    h = x.reshape(t, d)
    for i in range(depth):
        h = _ffn(h, ffn1_norm[i][None], ffn1_w13b[i], ffn1_w2b[i])
        main, gc, gr = _proj(h, mix_norm[i][None], w_main[i], wg[i], wgt[i], bias_c[i], bias_r[i])
        main3 = main.reshape(bsz, seq, -1)
        gc3 = gc.reshape(bsz, seq, GATE_LANES)
        hf = _mlstm_dir(main3, gc3, gr, off, False)
        ya = _mlstm_dir(main3, gc3, gr, off, True, h_fwd=hf, norm_g=mlstm_norm[i][None])
        xs3, bc3 = _conv(main3, conv_w[i], conv_b[i][None], off)
        yf = _ssd_dir(xs3, bc3, gc3, gr, alog_r[i], alog_c[i], False)
        yb = _ssd_dir(xs3, bc3, gc3, gr, alog_r[i], alog_c[i], True, main3=main3, off=off, y_fwd=yf,
                      d_skip=d_skip[i], norm_g=ssm_norm[i][None])
        q4, k4, v4 = _mla_prep(main3, tab3, mla_q_norm[i][None], mla_kv_norm[i][None], wq[i], wkv[i], off)
        yc = _attention(q4, k4, v4)
        merged = _merge(ya.reshape(t, -1), yb.reshape(t, -1), yc.reshape(t, -1), w_branch_b[i], main, off, d)
        h = _mm_res(merged, w_out_b[i], h)
        h = _ffn(h, ffn2_norm[i][None], ffn2_w13b[i], ffn2_w2b[i])
        h = _ple(h, ple_norm[i][None], w_pg_b[i], p[i].reshape(t, -1), w_pp_b[i])
    return _final_norm(h, final_norm[None]).reshape(bsz, seq, d)
```

```python
import functools
import math

import jax
import jax.numpy as jnp
from jax import lax
from jax.experimental import pallas as pl
from jax.experimental.pallas import tpu as pltpu

F32 = jnp.float32
BF16 = jnp.bfloat16

HALF_STEP = 0.5
NORM_EPS = 1e-6
CHUNK = 128
N_BRANCH = 3
A_HEADS = 4
A_QK = 128
A_V = 256
B_HEADS = 16
B_HEADDIM = 64
B_INNER = B_HEADS * B_HEADDIM
B_STATE = 128
B_GROUPS = 2
CONV_K = 5
C_HEADS = 8
C_Q_RANK = 512
C_KV_RANK = 512
C_NOPE = 128
C_ROPE = 64
C_V = 128
C_QK_PAD = 256
C_V_PAD = 256
ROPE_THETA = 10000.0
BRANCH_W = 1024
A_QK_W = A_HEADS * A_QK
A_V_W = A_HEADS * A_V
B_BC_W = 2 * B_GROUPS * B_STATE
N_GATE_ROWS = 2 * A_HEADS + 2 * A_HEADS + 2 * B_HEADS
GATE_LANES = 128
DT_COL0 = 4 * A_HEADS
SCAN_CHUNKS_PER_STEP = 4
VMEM_LIMIT = 56 * 1024 * 1024
HI = lax.Precision.HIGHEST
NT_DIMS = (((1,), (1,)), ((), ()))
TN_DIMS = (((0,), (0,)), ((), ()))


def _cparams(*sem):
    return pltpu.CompilerParams(dimension_semantics=sem, vmem_limit_bytes=VMEM_LIMIT)


def _tile(n, pref):
    t = min(n, pref)
    while n % t:
        t //= 2
    return t


def _softplus(x):
    return jnp.maximum(x, 0.0) + jnp.log1p(jnp.exp(-jnp.abs(x)))


def _sigmoid(x):
    return 1.0 / (1.0 + jnp.exp(-x))


def _rms(x, g):
    return x * lax.rsqrt(jnp.mean(x * x, axis=-1, keepdims=True) + NORM_EPS) * g


def _layout(d_model):
    off = {}
    o = 0
    for name, w in (("a_v", A_V_W), ("a_o", A_V_W), ("b_z", B_INNER), ("b_x", B_INNER),
                    ("gate", N_BRANCH * d_model), ("a_q", A_QK_W), ("a_k", A_QK_W),
                    ("b_bc", B_BC_W), ("c_q", C_Q_RANK), ("c_kv", C_KV_RANK), ("c_kr", 2 * C_ROPE)):
        off[name] = o
        o += w
    return off, o


def _ffn_kernel(x_ref, g_ref, w1_ref, w3_ref, w2_ref, o_ref, xn_ref):
    f = pl.program_id(1)

    @pl.when(f == 0)
    def _():
        xn_ref[...] = _rms(x_ref[...], g_ref[...]).astype(BF16)
        o_ref[...] = jnp.zeros_like(o_ref)

    xn = xn_ref[...]
    a = jnp.dot(xn, w1_ref[...], preferred_element_type=F32)
    b = jnp.dot(xn, w3_ref[...], preferred_element_type=F32)
    hid = (a * _sigmoid(a) * b).astype(BF16)
    o_ref[...] += jnp.dot(hid, w2_ref[...], preferred_element_type=F32)

    @pl.when(f == pl.num_programs(1) - 1)
    def _():
        o_ref[...] = x_ref[...] + HALF_STEP * o_ref[...]


def _ffn(h, g, w13, w2, li):
    t, d = h.shape
    d_ff = w2.shape[1]
    tm = _tile(t, 1024)
    tf = _tile(d_ff, 512)
    nf = d_ff // tf
    return pl.pallas_call(
        _ffn_kernel,
        out_shape=jax.ShapeDtypeStruct((t, d), F32),
        grid=(t // tm, nf),
        in_specs=[pl.BlockSpec((tm, d), lambda i, f: (i, 0)),
                  pl.BlockSpec((None, 1, d), lambda i, f: (li, 0, 0)),
                  pl.BlockSpec((None, d, tf), lambda i, f: (li, 0, f)),
                  pl.BlockSpec((None, d, tf), lambda i, f: (li, 0, nf + f)),
                  pl.BlockSpec((None, tf, d), lambda i, f: (li, f, 0))],
        out_specs=pl.BlockSpec((tm, d), lambda i, f: (i, 0)),
        scratch_shapes=[pltpu.VMEM((tm, d), BF16)],
        compiler_params=_cparams("parallel", "arbitrary"),
        name="ffn",
    )(h, g, w13, w13, w2)


def _gate_act(z, idx):
    return jnp.where(idx < 2 * A_HEADS, z,
                     jnp.where(idx < 4 * A_HEADS, -_softplus(-z), _softplus(z)))


def _proj_kernel(x_ref, g_ref, w_ref, wg_ref, wgt_ref, bc_ref, br_ref, o_ref, gc_ref, gr_ref, xn_ref):
    @pl.when(pl.program_id(1) == 0)
    def _():
        xn = _rms(x_ref[...], g_ref[...]).astype(BF16)
        xn_ref[...] = xn
        zc = jnp.dot(xn, wg_ref[...], preferred_element_type=F32) + bc_ref[...]
        gc_ref[...] = _gate_act(zc, lax.broadcasted_iota(jnp.int32, zc.shape, 1))
        zr = lax.dot_general(wgt_ref[...], xn, NT_DIMS, preferred_element_type=F32) + br_ref[...]
        gr_ref[...] = _gate_act(zr, lax.broadcasted_iota(jnp.int32, zr.shape, 0))

    o_ref[...] = jnp.dot(xn_ref[...], w_ref[...], preferred_element_type=F32).astype(o_ref.dtype)


def _proj(h, g, w_main, wg, wgt, bias_c, bias_r, li):
    t, d = h.shape
    n = w_main.shape[2]
    tm = _tile(t, 1024)
    tn = _tile(n, 1024)
    return pl.pallas_call(
        _proj_kernel,
        out_shape=(jax.ShapeDtypeStruct((t, n), BF16),
                   jax.ShapeDtypeStruct((t, GATE_LANES), F32),
                   jax.ShapeDtypeStruct((N_GATE_ROWS, t), F32)),
        grid=(t // tm, n // tn),
        in_specs=[pl.BlockSpec((tm, d), lambda i, j: (i, 0)),
                  pl.BlockSpec((None, 1, d), lambda i, j: (li, 0, 0)),
                  pl.BlockSpec((None, d, tn), lambda i, j: (li, 0, j)),
                  pl.BlockSpec((None, d, GATE_LANES), lambda i, j: (li, 0, 0)),
                  pl.BlockSpec((None, N_GATE_ROWS, d), lambda i, j: (li, 0, 0)),
                  pl.BlockSpec((None, 1, GATE_LANES), lambda i, j: (li, 0, 0)),
                  pl.BlockSpec((None, N_GATE_ROWS, 1), lambda i, j: (li, 0, 0))],
        out_specs=(pl.BlockSpec((tm, tn), lambda i, j: (i, j)),
                   pl.BlockSpec((tm, GATE_LANES), lambda i, j: (i, 0)),
                   pl.BlockSpec((N_GATE_ROWS, tm), lambda i, j: (0, i))),
        scratch_shapes=[pltpu.VMEM((tm, d), BF16)],
        compiler_params=_cparams("parallel", "arbitrary"),
        name="in_proj",
    )(h, g, w_main, wg, wgt, bias_c, bias_r)


def _chunk_mask(reverse):
    ti = lax.broadcasted_iota(jnp.int32, (CHUNK, CHUNK), 0)
    si = lax.broadcasted_iota(jnp.int32, (CHUNK, CHUNK), 1)
    return (si >= ti) if reverse else (si <= ti)


def _mlstm_kernel(*refs, reverse, final):
    if final:
        (q_ref, k_ref, v_ref, gc_ref, gr_ref, o_ref, hf_ref, ng_ref,
         out_ref, c_ref, n_ref, m_ref) = refs
    else:
        q_ref, k_ref, v_ref, gc_ref, gr_ref, out_ref, c_ref, n_ref, m_ref = refs

    @pl.when(pl.program_id(1) == 0)
    def _():
        c_ref[...] = jnp.zeros_like(c_ref)
        n_ref[...] = jnp.zeros_like(n_ref)
        m_ref[...] = jnp.zeros_like(m_ref)

    allowed = _chunk_mask(reverse)
    amat = allowed.astype(F32)
    scale = A_QK ** -0.5
    d = 1 if reverse else 0
    n_sub = q_ref.shape[1] // CHUNK
    m_st = [m_ref[hd][:, 0:1] for hd in range(A_HEADS)]
    c_st = [c_ref[hd] for hd in range(A_HEADS)]
    n_st = [n_ref[hd] for hd in range(A_HEADS)]
    for cc in (reversed(range(n_sub)) if reverse else range(n_sub)):
        rows = slice(cc * CHUNK, (cc + 1) * CHUNK)
        gc = gc_ref[0, rows, :]
        gr = gr_ref[:, rows]
        cum_c = jnp.dot(amat, gc, precision=HI, preferred_element_type=F32)
        cum_r = lax.dot_general(gr, amat, NT_DIMS, precision=HI, preferred_element_type=F32)
        for hd in range(A_HEADS):
            ii = d * A_HEADS + hd
            fi = 2 * A_HEADS + d * A_HEADS + hd
            li_c = gc[:, ii:ii + 1]
            li_r = gr[ii:ii + 1, :]
            b_c = cum_c[:, fi:fi + 1]
            b_r = cum_r[fi:fi + 1, :]
            g_tot = jnp.sum(gc[:, fi:fi + 1], axis=0, keepdims=True)
            m0, c0, n0 = m_st[hd], c_st[hd], n_st[hd]
            qh = q_ref[0, rows, hd * A_QK:(hd + 1) * A_QK]
            kh = k_ref[0, rows, hd * A_QK:(hd + 1) * A_QK]
            vh = v_ref[0, rows, hd * A_V:(hd + 1) * A_V]
            kf = kh.astype(F32)
            w_state = g_tot - b_c + li_c
            m_loc = jnp.max(w_state, axis=0, keepdims=True)
            es = jnp.exp(w_state - m_loc) * scale
            ke = kf * es
            c_loc = lax.dot_general(ke.astype(BF16), vh, TN_DIMS, preferred_element_type=F32)
            n_loc = jnp.sum(ke, axis=0, keepdims=True)
            log_d = jnp.where(allowed, b_c - b_r + li_r, -jnp.inf)
            log_inter = b_c + m0
            m_t = jnp.maximum(log_inter, jnp.max(log_d, axis=1, keepdims=True))
            qk = lax.dot_general(qh, kh, NT_DIMS, preferred_element_type=F32)
            s_qk = qk * (jnp.exp(log_d - m_t) * scale)
            e_inter = jnp.exp(log_inter - m_t)
            num = (jnp.dot(s_qk.astype(BF16), vh, preferred_element_type=F32)
                   + e_inter * jnp.dot(qh, c0.astype(BF16), preferred_element_type=F32))
            den = (jnp.sum(s_qk, axis=1, keepdims=True)
                   + e_inter * jnp.sum(qh.astype(F32) * n0, axis=1, keepdims=True))
            hout = num / jnp.maximum(jnp.abs(den), jnp.exp(-m_t))
            m_new = jnp.maximum(g_tot + m0, m_loc)
            a_prev = jnp.exp(g_tot + m0 - m_new)
            a_loc = jnp.exp(m_loc - m_new)
            c_st[hd] = a_prev * c0 + a_loc * c_loc
            n_st[hd] = a_prev * n0 + a_loc * n_loc
            m_st[hd] = m_new
            sl = slice(hd * A_V, (hd + 1) * A_V)
            if final:
                htot = hout + hf_ref[0, rows, sl]
                hn = htot * lax.rsqrt(jnp.mean(htot * htot, axis=-1, keepdims=True) + NORM_EPS)
                out_ref[0, rows, sl] = (_sigmoid(o_ref[0, rows, sl].astype(F32)) * hn
                                        * ng_ref[:, sl]).astype(out_ref.dtype)
            else:
                out_ref[0, rows, sl] = hout
    for hd in range(A_HEADS):
        c_ref[hd] = c_st[hd]
        n_ref[hd] = n_st[hd]
        m_ref[hd] = jnp.broadcast_to(m_st[hd], (1, GATE_LANES))


def _mlstm_dir(main3, gc3, gr, off, reverse, h_fwd=None, norm_g=None, li=0):
    bsz, seq, _ = main3.shape
    rows = SCAN_CHUNKS_PER_STEP * CHUNK
    nc = seq // rows
    final = h_fwd is not None

    def cidx(c):
        return (nc - 1 - c) if reverse else c

    def col(name, width):
        blk = off[name] // width
        return pl.BlockSpec((1, rows, width), lambda b, c: (b, cidx(c), blk))

    in_specs = [col("a_q", A_QK_W), col("a_k", A_QK_W), col("a_v", A_V_W),
                pl.BlockSpec((1, rows, GATE_LANES), lambda b, c: (b, cidx(c), 0)),
                pl.BlockSpec((N_GATE_ROWS, rows), lambda b, c: (0, b * nc + cidx(c)))]
    args = [main3, main3, main3, gc3, gr]
    if final:
        in_specs += [col("a_o", A_V_W),
                     pl.BlockSpec((1, rows, A_V_W), lambda b, c: (b, cidx(c), 0)),
                     pl.BlockSpec((None, 1, A_V_W), lambda b, c: (li, 0, 0))]
        args += [main3, h_fwd, norm_g]
    return pl.pallas_call(
        functools.partial(_mlstm_kernel, reverse=reverse, final=final),
        out_shape=jax.ShapeDtypeStruct((bsz, seq, A_V_W), BF16 if final else F32),
        grid=(bsz, nc),
        in_specs=in_specs,
        out_specs=pl.BlockSpec((1, rows, A_V_W), lambda b, c: (b, cidx(c), 0)),
        scratch_shapes=[pltpu.VMEM((A_HEADS, A_QK, A_V), F32),
                        pltpu.VMEM((A_HEADS, 1, A_QK), F32),
                        pltpu.VMEM((A_HEADS, 1, GATE_LANES), F32)],
        compiler_params=_cparams("parallel", "arbitrary"),
        name="mlstm_bwd" if reverse else "mlstm_fwd",
    )(*args)


def _conv_kernel(xc_ref, xp_ref, xn_ref, bc_ref, bp_ref, bn_ref, w_ref, b_ref, ox_ref, obc_ref, sx_ref, sbc_ref):
    i = pl.program_id(1)
    first = i == 0
    last = i == pl.num_programs(1) - 1
    pad = (CONV_K - 1) // 2

    def run(cur_ref, prev_ref, next_ref, scr_ref, out_ref, c0):
        tb, w = cur_ref.shape[1], cur_ref.shape[2]
        scr_ref[0:8, :] = jnp.where(first, 0.0, prev_ref[0].astype(F32))
        scr_ref[8:8 + tb, :] = cur_ref[0].astype(F32)
        scr_ref[8 + tb:16 + tb, :] = jnp.where(last, 0.0, next_ref[0].astype(F32))
        acc = jnp.zeros((tb, w), F32) + b_ref[:, c0:c0 + w]
        for j in range(CONV_K):
            acc = acc + scr_ref[8 - pad + j:8 - pad + j + tb, :] * w_ref[j:j + 1, c0:c0 + w]
        out_ref[0] = (acc * _sigmoid(acc)).astype(out_ref.dtype)

    run(xc_ref, xp_ref, xn_ref, sx_ref, ox_ref, 0)
    run(bc_ref, bp_ref, bn_ref, sbc_ref, obc_ref, B_INNER)


def _conv(main3, conv_w, conv_b, off, li):
    bsz, seq, _ = main3.shape
    tb = _tile(seq, 512)
    nb = seq // tb
    r8 = tb // 8
    last8 = seq // 8 - 1

    def trio(name, width):
        blk = off[name] // width
        return [pl.BlockSpec((1, tb, width), lambda b, i: (b, i, blk)),
                pl.BlockSpec((1, 8, width), lambda b, i: (b, jnp.maximum(i * r8 - 1, 0), blk)),
                pl.BlockSpec((1, 8, width), lambda b, i: (b, jnp.minimum((i + 1) * r8, last8), blk))]

    wtot = B_INNER + B_BC_W
    return pl.pallas_call(
        _conv_kernel,
        out_shape=(jax.ShapeDtypeStruct((bsz, seq, B_INNER), BF16),
                   jax.ShapeDtypeStruct((bsz, seq, B_BC_W), BF16)),
        grid=(bsz, nb),
        in_specs=trio("b_x", B_INNER) + trio("b_bc", B_BC_W) + [
            pl.BlockSpec((None, CONV_K, wtot), lambda b, i: (li, 0, 0)),
            pl.BlockSpec((None, 1, wtot), lambda b, i: (li, 0, 0))],
        out_specs=(pl.BlockSpec((1, tb, B_INNER), lambda b, i: (b, i, 0)),
                   pl.BlockSpec((1, tb, B_BC_W), lambda b, i: (b, i, 0))),
        scratch_shapes=[pltpu.VMEM((tb + 16, B_INNER), F32), pltpu.VMEM((tb + 16, B_BC_W), F32)],
        compiler_params=_cparams("parallel", "parallel"),
        name="dwconv_silu",
    )(main3, main3, main3, main3, main3, main3, conv_w, conv_b)


def _ssd_kernel(*refs, reverse, final):
    if final:
        (xs_ref, bc_ref, gc_ref, gr_ref, alr_ref, alc_ref, z_ref, yf_ref, dsk_ref, ng_ref,
         out_ref, s_ref, y_ref) = refs
    else:
        xs_ref, bc_ref, gc_ref, gr_ref, alr_ref, alc_ref, out_ref, s_ref, y_ref = refs

    @pl.when(pl.program_id(1) == 0)
    def _():
        s_ref[...] = jnp.zeros_like(s_ref)

    d = 1 if reverse else 0
    col0 = DT_COL0 + d * B_HEADS
    allowed = _chunk_mask(reverse)
    amat = allowed.astype(F32)
    lane = lax.broadcasted_iota(jnp.int32, (1, GATE_LANES), 1)
    a_row = jnp.where((lane >= col0) & (lane < col0 + B_HEADS), -jnp.exp(alr_ref[...]), 0.0)
    row = lax.broadcasted_iota(jnp.int32, (N_GATE_ROWS, 1), 0)
    a_col = jnp.where((row >= col0) & (row < col0 + B_HEADS), -jnp.exp(alc_ref[...]), 0.0)
    er = lax.broadcasted_iota(jnp.int32, (GATE_LANES, B_INNER), 0)
    ec = lax.broadcasted_iota(jnp.int32, (GATE_LANES, B_INNER), 1)
    expand = (er == col0 + (ec >> (B_HEADDIM.bit_length() - 1))).astype(F32)
    expand_b = expand.astype(BF16)

    def spread(t):
        return jnp.dot(t.astype(BF16), expand_b, preferred_element_type=F32)

    gw = B_INNER // B_GROUPS
    hpg = B_HEADS // B_GROUPS
    n_sub = xs_ref.shape[1] // CHUNK
    s_st = [s_ref[g] for g in range(B_GROUPS)]
    for cc in (reversed(range(n_sub)) if reverse else range(n_sub)):
        rows = slice(cc * CHUNK, (cc + 1) * CHUNK)
        gc = gc_ref[0, rows, :]
        gr = gr_ref[:, rows]
        dta_c = gc * a_row
        dta_r = gr * a_col
        acs_c = jnp.dot(amat, dta_c, precision=HI, preferred_element_type=F32)
        acs_r = lax.dot_general(dta_r, amat, NT_DIMS, precision=HI, preferred_element_type=F32)
        tot = jnp.sum(dta_c, axis=0, keepdims=True)
        dt_x = spread(gc)
        to_end_x = spread(jnp.exp(tot - acs_c))
        from_start_x = spread(jnp.exp(acs_c))
        chunk_decay_x = jnp.dot(jnp.broadcast_to(jnp.exp(tot), (8, GATE_LANES)), expand, precision=HI,
                                preferred_element_type=F32)[0:1]
        xs = xs_ref[0, rows, :].astype(F32)
        xdt = xs * dt_x
        xdt_b = xdt.astype(BF16)
        xend_b = (xdt * to_end_x).astype(BF16)
        for g in range(B_GROUPS):
            bm = bc_ref[0, rows, g * B_STATE:(g + 1) * B_STATE]
            cm = bc_ref[0, rows, (B_GROUPS + g) * B_STATE:(B_GROUPS + g + 1) * B_STATE]
            gs = slice(g * gw, (g + 1) * gw)
            s0 = s_st[g]
            y_ref[rows, gs] = jnp.dot(cm, s0.astype(BF16), preferred_element_type=F32) * from_start_x[:, gs]
            states = lax.dot_general(bm, xend_b[:, gs], TN_DIMS, preferred_element_type=F32)
            s_st[g] = chunk_decay_x[:, gs] * s0 + states
            cb = lax.dot_general(cm, bm, NT_DIMS, preferred_element_type=F32)
            for hh in range(hpg):
                ci = col0 + g * hpg + hh
                seg = acs_c[:, ci:ci + 1] - acs_r[ci:ci + 1, :]
                dec = jnp.exp(jnp.where(allowed, seg, -jnp.inf))
                hs = slice((g * hpg + hh) * B_HEADDIM, (g * hpg + hh + 1) * B_HEADDIM)
                y_ref[rows, hs] += jnp.dot((cb * dec).astype(BF16), xdt_b[:, hs], preferred_element_type=F32)
    for g in range(B_GROUPS):
        s_ref[g] = s_st[g]
    if final:
        y = y_ref[...] + yf_ref[0] + dsk_ref[...] * xs_ref[0].astype(F32)
        zf = z_ref[0].astype(F32)
        out_ref[0] = _rms(y * (zf * _sigmoid(zf)), ng_ref[...]).astype(out_ref.dtype)
    else:
        out_ref[0] = y_ref[...]


def _ssd_dir(xs3, bc3, gc3, gr, alog_r, alog_c, reverse, li, main3=None, off=None, y_fwd=None, d_skip=None,
             norm_g=None):
    bsz, seq, _ = xs3.shape
    rows = SCAN_CHUNKS_PER_STEP * CHUNK
    nc = seq // rows
    final = y_fwd is not None

    def cidx(c):
        return (nc - 1 - c) if reverse else c

    def seq_spec(width, blk=0):
        return pl.BlockSpec((1, rows, width), lambda b, c: (b, cidx(c), blk))

    def const_spec(shape):
        return pl.BlockSpec((None,) + shape, lambda b, c: (li, 0, 0))

    in_specs = [seq_spec(B_INNER), seq_spec(B_BC_W), seq_spec(GATE_LANES),
                pl.BlockSpec((N_GATE_ROWS, rows), lambda b, c: (0, b * nc + cidx(c))),
                const_spec((1, GATE_LANES)), const_spec((N_GATE_ROWS, 1))]
    args = [xs3, bc3, gc3, gr, alog_r, alog_c]
    if final:
        in_specs += [seq_spec(B_INNER, off["b_z"] // B_INNER), seq_spec(B_INNER),
                     const_spec((1, B_INNER)), const_spec((1, B_INNER))]
        args += [main3, y_fwd, d_skip, norm_g]
    return pl.pallas_call(
        functools.partial(_ssd_kernel, reverse=reverse, final=final),
        out_shape=jax.ShapeDtypeStruct((bsz, seq, B_INNER), BF16 if final else F32),
        grid=(bsz, nc),
        in_specs=in_specs,
        out_specs=seq_spec(B_INNER),
        scratch_shapes=[pltpu.VMEM((B_GROUPS, B_STATE, B_INNER // B_GROUPS), F32),
                        pltpu.VMEM((rows, B_INNER), F32)],
        compiler_params=_cparams("parallel", "arbitrary"),
        name="ssd_bwd" if reverse else "ssd_fwd",
    )(*args)


def _rope_kernel(pos_ref, tab_ref):
    half = C_ROPE // 2
    lane = lax.broadcasted_iota(jnp.int32, (1, 4 * half), 1)
    fr = (lane & (half - 1)).astype(F32)
    inv_freq = jnp.exp(fr * (-math.log(ROPE_THETA) / half))
    ang = pos_ref[...].astype(F32) * inv_freq
    sn = jnp.sin(ang)
    tab_ref[...] = jnp.where(lane < 2 * half, jnp.cos(ang), jnp.where(lane < 3 * half, -sn, sn))


def _rope_table(pos_col):
    t = pos_col.shape[0]
    tm = _tile(t, 1024)
    return pl.pallas_call(
        _rope_kernel,
        out_shape=jax.ShapeDtypeStruct((t, 2 * C_ROPE), F32),
        grid=(t // tm,),
        in_specs=[pl.BlockSpec((tm, 1), lambda i: (i, 0))],
        out_specs=pl.BlockSpec((tm, 2 * C_ROPE), lambda i: (i, 0)),
        compiler_params=_cparams("parallel"),
        name="rope_table",
    )(pos_col)


def _mla_prep_kernel(cq_ref, ckv_ref, ckr_ref, tab_ref, qg_ref, kvg_ref, wq_ref, wkv_ref, q_ref, k_ref, v_ref):
    tab = tab_ref[0]
    lane = lax.broadcasted_iota(jnp.int32, tab.shape, 1)

    def rope(t, zero_upper):
        p = t * tab
        r = p + pltpu.roll(p, C_ROPE, axis=1)
        return jnp.where(lane < C_ROPE, r, 0.0) if zero_upper else r

    cqn = _rms(cq_ref[0].astype(F32), qg_ref[...]).astype(BF16)
    ckvn = _rms(ckv_ref[0].astype(F32), kvg_ref[...]).astype(BF16)
    q_all = jnp.dot(cqn, wq_ref[...], preferred_element_type=F32)
    kv_all = jnp.dot(ckvn, wkv_ref[...], preferred_element_type=F32)
    k_rot = rope(ckr_ref[0].astype(F32), False).astype(BF16)
    qscale = (C_NOPE + C_ROPE) ** -0.5 * math.log2(math.e)
    ones_col = (lane == 0).astype(BF16)
    for hd in range(C_HEADS):
        q0 = hd * C_QK_PAD
        q_ref[0, hd, :, 0:C_NOPE] = (q_all[:, q0:q0 + C_NOPE] * qscale).astype(BF16)
        q_ref[0, hd, :, C_NOPE:C_QK_PAD] = (rope(q_all[:, q0 + C_NOPE:q0 + C_QK_PAD], True) * qscale).astype(BF16)
        k_ref[0, hd, :, 0:C_NOPE] = kv_all[:, hd * C_NOPE:(hd + 1) * C_NOPE].astype(BF16)
        k_ref[0, hd, :, C_NOPE:C_QK_PAD] = k_rot
        v0 = C_HEADS * C_NOPE + hd * C_V
        v_ref[0, hd, :, 0:C_V] = kv_all[:, v0:v0 + C_V].astype(BF16)
        v_ref[0, hd, :, C_V:C_V_PAD] = ones_col


def _mla_prep(main3, tab3, q_g, kv_g, wq, wkv, off, li):
    bsz, seq, _ = main3.shape
    tm = _tile(seq, 512)

    def col(name, width):
        blk = off[name] // width
        return pl.BlockSpec((1, tm, width), lambda b, i: (b, i, blk))

    def const_spec(shape):
        return pl.BlockSpec((None,) + tuple(shape), lambda b, i: (li, 0, 0))

    def head_spec(width):
        return pl.BlockSpec((1, C_HEADS, tm, width), lambda b, i: (b, 0, i, 0))

    return pl.pallas_call(
        _mla_prep_kernel,
        out_shape=(jax.ShapeDtypeStruct((bsz, C_HEADS, seq, C_QK_PAD), BF16),
                   jax.ShapeDtypeStruct((bsz, C_HEADS, seq, C_QK_PAD), BF16),
                   jax.ShapeDtypeStruct((bsz, C_HEADS, seq, C_V_PAD), BF16)),
        grid=(bsz, seq // tm),
        in_specs=[col("c_q", C_Q_RANK), col("c_kv", C_KV_RANK), col("c_kr", 2 * C_ROPE),
                  pl.BlockSpec((1, tm, 2 * C_ROPE), lambda b, i: (b, i, 0)),
                  const_spec((1, C_Q_RANK)), const_spec((1, C_KV_RANK)),
                  const_spec(wq.shape[1:]), const_spec(wkv.shape[1:])],
        out_specs=(head_spec(C_QK_PAD), head_spec(C_QK_PAD), head_spec(C_V_PAD)),
        compiler_params=_cparams("parallel", "parallel"),
        name="mla_prep",
    )(main3, main3, main3, tab3, q_g, kv_g, wq, wkv)


def _attn_kernel(q_ref, k_ref, v_ref, o_ref, *, tk):
    q = q_ref[0, 0]
    nk = k_ref.shape[2] // tk
    m = None
    acc = None
    for j in range(nk):
        kb = k_ref[0, 0, j * tk:(j + 1) * tk, :]
        vb = v_ref[0, 0, j * tk:(j + 1) * tk, :]
        s = lax.dot_general(q, kb, NT_DIMS, preferred_element_type=F32)
        m_blk = jnp.max(s, axis=1, keepdims=True)
        m_new = m_blk if m is None else jnp.maximum(m, m_blk)
        pv = jnp.dot(jnp.exp2(s - m_new).astype(BF16), vb, preferred_element_type=F32)
        acc = pv if acc is None else jnp.exp2(m - m_new) * acc + pv
        m = m_new
    o_ref[0] = (acc[:, 0:C_V] / acc[:, C_V:C_V + 1]).astype(o_ref.dtype)


def _attention(q4, k4, v4):
    bsz, nh, seq, _ = q4.shape
    tq = _tile(seq, 512)
    tk = _tile(seq, 512)
    return pl.pallas_call(
        functools.partial(_attn_kernel, tk=tk),
        out_shape=jax.ShapeDtypeStruct((bsz, seq, nh * C_V), BF16),
        grid=(bsz, nh, seq // tq),
        in_specs=[pl.BlockSpec((1, 1, tq, C_QK_PAD), lambda b, h, i: (b, h, i, 0)),
                  pl.BlockSpec((1, 1, seq, C_QK_PAD), lambda b, h, i: (b, h, 0, 0)),
                  pl.BlockSpec((1, 1, seq, C_V_PAD), lambda b, h, i: (b, h, 0, 0))],
        out_specs=pl.BlockSpec((1, tq, C_V), lambda b, h, i: (b, i, h)),
        compiler_params=_cparams("parallel", "parallel", "parallel"),
        name="mla_attention",
    )(q4, k4, v4)


def _merge_kernel(ya_ref, yb_ref, yc_ref, wb_ref, g0_ref, g1_ref, g2_ref, o_ref):
    acc = None
    for y_ref, g_ref, k in ((ya_ref, g0_ref, 0), (yb_ref, g1_ref, 1), (yc_ref, g2_ref, 2)):
        t = _sigmoid(g_ref[...].astype(F32)) * jnp.dot(y_ref[...], wb_ref[k], preferred_element_type=F32)
        acc = t if acc is None else acc + t
    o_ref[...] = acc.astype(o_ref.dtype)


def _merge(ya, yb, yc, wb, main, off, d_model, li):
    t = ya.shape[0]
    tm = _tile(t, 1024)
    tn = _tile(d_model, 512)
    gblk = off["gate"] // tn
    per = d_model // tn

    def gate_spec(k):
        return pl.BlockSpec((tm, tn), lambda i, j: (i, gblk + k * per + j))

    y_spec = pl.BlockSpec((tm, BRANCH_W), lambda i, j: (i, 0))
    return pl.pallas_call(
        _merge_kernel,
        out_shape=jax.ShapeDtypeStruct((t, d_model), BF16),
        grid=(t // tm, per),
        in_specs=[y_spec, y_spec, y_spec,
                  pl.BlockSpec((None, N_BRANCH, BRANCH_W, tn), lambda i, j: (li, 0, 0, j)),
                  gate_spec(0), gate_spec(1), gate_spec(2)],
        out_specs=pl.BlockSpec((tm, tn), lambda i, j: (i, j)),
        compiler_params=_cparams("parallel", "parallel"),
        name="branch_merge",
    )(ya, yb, yc, wb, main, main, main)


def _mm_res_kernel(a_ref, w_ref, r_ref, o_ref):
    o_ref[...] = r_ref[...] + jnp.dot(a_ref[...], w_ref[...], preferred_element_type=F32)


def _mm_res(a, w, res, li):
    t, k = a.shape
    n = w.shape[2]
    tm = _tile(t, 1024)
    tn = _tile(n, 1024)
    return pl.pallas_call(
        _mm_res_kernel,
        out_shape=jax.ShapeDtypeStruct((t, n), F32),
        grid=(t // tm, n // tn),
        in_specs=[pl.BlockSpec((tm, k), lambda i, j: (i, 0)),
                  pl.BlockSpec((None, k, tn), lambda i, j: (li, 0, j)),
                  pl.BlockSpec((tm, tn), lambda i, j: (i, j))],
        out_specs=pl.BlockSpec((tm, tn), lambda i, j: (i, j)),
        compiler_params=_cparams("parallel", "parallel"),
        name="out_proj",
    )(a, w, res)


def _ple_kernel(x_ref, g_ref, wg_ref, p_ref, wp_ref, r_ref, o_ref, xn_ref):
    @pl.when(pl.program_id(1) == 0)
    def _():
        xn_ref[...] = _rms(x_ref[...], g_ref[...]).astype(BF16)

    gate = _sigmoid(jnp.dot(xn_ref[...], wg_ref[...], preferred_element_type=F32))
    emb = jnp.dot(p_ref[...].astype(BF16), wp_ref[...], preferred_element_type=F32)
    o_ref[...] = r_ref[...] + gate * emb


def _ple(h, g, wgate, p, wproj, li):
    t, d = h.shape
    pd = p.shape[2]
    tm = _tile(t, 1024)
    tn = _tile(d, 1024)
    return pl.pallas_call(
        _ple_kernel,
        out_shape=jax.ShapeDtypeStruct((t, d), F32),
        grid=(t // tm, d // tn),
        in_specs=[pl.BlockSpec((tm, d), lambda i, j: (i, 0)),
                  pl.BlockSpec((None, 1, d), lambda i, j: (li, 0, 0)),
                  pl.BlockSpec((None, d, tn), lambda i, j: (li, 0, j)),
                  pl.BlockSpec((None, tm, pd), lambda i, j: (li, i, 0)),
                  pl.BlockSpec((None, pd, tn), lambda i, j: (li, 0, j)),
                  pl.BlockSpec((tm, tn), lambda i, j: (i, j))],
        out_specs=pl.BlockSpec((tm, tn), lambda i, j: (i, j)),
        scratch_shapes=[pltpu.VMEM((tm, d), BF16)],
        compiler_params=_cparams("parallel", "arbitrary"),
        name="ple",
    )(h, g, wgate, p, wproj, h)


def _norm_kernel(x_ref, g_ref, o_ref):
    o_ref[...] = _rms(x_ref[...], g_ref[...])


def _final_norm(h, g):
    t, d = h.shape
    tm = _tile(t, 512)
    return pl.pallas_call(
        _norm_kernel,
        out_shape=jax.ShapeDtypeStruct((t, d), F32),
        grid=(t // tm,),
        in_specs=[pl.BlockSpec((tm, d), lambda i: (i, 0)), pl.BlockSpec((1, d), lambda i: (0, 0))],
        out_specs=pl.BlockSpec((tm, d), lambda i: (i, 0)),
        compiler_params=_cparams("parallel"),
        name="final_norm",
    )(h, g)


def _prep_w_in(w_in, d_model):
    sizes = (A_QK_W, A_QK_W, A_V_W, A_V_W, 2 * A_HEADS, 2 * A_HEADS, B_INNER, B_INNER + B_BC_W,
             2 * B_HEADS, C_Q_RANK, C_KV_RANK, C_ROPE, N_BRANCH * d_model)
    cuts, o = [], 0
    for s in sizes:
        cuts.append((o, o + s))
        o += s
    (a_q, a_k, a_v, a_o, a_ig, a_fg, b_z, b_xbc, b_dt, c_q, c_kv, c_kr, gate) = [w_in[..., lo:hi] for lo, hi in cuts]
    half = C_ROPE // 2
    kr_swapped = jnp.concatenate([c_kr[..., half:], c_kr[..., :half]], axis=-1)
    pieces = {"a_v": a_v, "a_o": a_o, "b_z": b_z, "b_x": b_xbc[..., :B_INNER], "gate": gate, "a_q": a_q,
              "a_k": a_k, "b_bc": b_xbc[..., B_INNER:], "c_q": c_q, "c_kv": c_kv,
              "c_kr": jnp.concatenate([c_kr, kr_swapped], axis=-1)}
    off, n1 = _layout(d_model)
    order = sorted(off, key=off.get)
    n_pad = -n1 % 1024
    cols = [pieces[k] for k in order]
    if n_pad:
        cols.append(jnp.zeros(w_in.shape[:-1] + (n_pad,), w_in.dtype))
    w_main = jnp.concatenate(cols, axis=-1).astype(BF16)
    gates = jnp.concatenate([a_ig, a_fg, b_dt], axis=-1)
    wg = jnp.concatenate([gates, jnp.zeros(w_in.shape[:-1] + (GATE_LANES - N_GATE_ROWS,), w_in.dtype)],
                         axis=-1).astype(BF16)
    wgt = jnp.swapaxes(gates, -1, -2).astype(BF16)
    return w_main, wg, wgt, off


def _prep_mla(w_uq, w_ukv):
    nl, rq, _ = w_uq.shape
    half = C_ROPE // 2
    wq = w_uq.reshape(nl, rq, C_HEADS, C_NOPE + C_ROPE)
    t1 = wq[..., C_NOPE:C_NOPE + half]
    t2 = wq[..., C_NOPE + half:]
    wq = jnp.concatenate([wq[..., :C_NOPE], t1, t2, t2, t1], axis=-1).reshape(nl, rq, C_HEADS * C_QK_PAD)
    wkv = w_ukv.reshape(nl, w_ukv.shape[1], C_HEADS, C_NOPE + C_V)
    wkv = jnp.concatenate([wkv[..., :C_NOPE].reshape(nl, -1, C_HEADS * C_NOPE),
                           wkv[..., C_NOPE:].reshape(nl, -1, C_HEADS * C_V)], axis=-1)
    return wq.astype(BF16), wkv.astype(BF16)


def kernel(x, p, positions, ffn1_norm, ffn1_w13, ffn1_w2, mix_norm, w_in, mlstm_b_igate, mlstm_b_fgate, mlstm_norm, conv_w, conv_b, ssm_a_log, ssm_dt_bias, ssm_d, ssm_norm, mla_q_norm, mla_kv_norm, mla_w_uq, mla_w_ukv, w_branch, w_out, ffn2_norm, ffn2_w13, ffn2_w2, ple_norm, w_ple_gate, w_ple_proj, final_norm):
    bsz, seq, d = x.shape
    depth = w_in.shape[0]
    t = bsz * seq
    nl = depth

    w_main, wg, wgt, off = _prep_w_in(w_in, d)
    wq, wkv = _prep_mla(mla_w_uq, mla_w_ukv)
    ffn1_w13b, ffn1_w2b = ffn1_w13.astype(BF16), ffn1_w2.astype(BF16)
    ffn2_w13b, ffn2_w2b = ffn2_w13.astype(BF16), ffn2_w2.astype(BF16)
    w_branch_b, w_out_b = w_branch.astype(BF16), w_out.astype(BF16)
    w_pg_b, w_pp_b = w_ple_gate.astype(BF16), w_ple_proj.astype(BF16)
    gate_bias = jnp.concatenate([mlstm_b_igate.reshape(nl, -1), mlstm_b_fgate.reshape(nl, -1),
                                 ssm_dt_bias.reshape(nl, -1)], axis=-1).astype(F32)
    bias_c = jnp.pad(gate_bias, ((0, 0), (0, GATE_LANES - N_GATE_ROWS)))[:, None, :]
    bias_r = gate_bias[:, :, None]
    alog = ssm_a_log.reshape(nl, -1).astype(F32)
    alog_r = jnp.pad(alog, ((0, 0), (DT_COL0, GATE_LANES - N_GATE_ROWS)))[:, None, :]
    alog_c = jnp.pad(alog, ((0, 0), (DT_COL0, 0)))[:, :, None]
    d_skip = jnp.repeat(ssm_d.astype(F32), B_HEADDIM, axis=-1)[:, None, :]
    row = lambda v: v[:, None, :]
    p3 = p.reshape(nl, t, -1)

    tab3 = _rope_table(positions.reshape(t, 1)).reshape(bsz, seq, 2 * C_ROPE)

    h = x.reshape(t, d)
    for i in range(depth):
        h = _ffn(h, row(ffn1_norm), ffn1_w13b, ffn1_w2b, i)
        main, gc, gr = _proj(h, row(mix_norm), w_main, wg, wgt, bias_c, bias_r, i)
        main3 = main.reshape(bsz, seq, -1)
        gc3 = gc.reshape(bsz, seq, GATE_LANES)
        hf = _mlstm_dir(main3, gc3, gr, off, False)
        ya = _mlstm_dir(main3, gc3, gr, off, True, h_fwd=hf, norm_g=row(mlstm_norm), li=i)
        xs3, bc3 = _conv(main3, conv_w, row(conv_b), off, i)
        yf = _ssd_dir(xs3, bc3, gc3, gr, alog_r, alog_c, False, i)
        yb = _ssd_dir(xs3, bc3, gc3, gr, alog_r, alog_c, True, i, main3=main3, off=off, y_fwd=yf,
                      d_skip=d_skip, norm_g=row(ssm_norm))
        q4, k4, v4 = _mla_prep(main3, tab3, row(mla_q_norm), row(mla_kv_norm), wq, wkv, off, i)
        yc = _attention(q4, k4, v4)
        merged = _merge(ya.reshape(t, -1), yb.reshape(t, -1), yc.reshape(t, -1), w_branch_b, main, off, d, i)
        h = _mm_res(merged, w_out_b, h, i)
        h = _ffn(h, row(ffn2_norm), ffn2_w13b, ffn2_w2b, i)
        h = _ple(h, row(ple_norm), w_pg_b, p3, w_pp_b, i)
    return _final_norm(h, final_norm[None]).reshape(bsz, seq, d)
```

```python
import functools
import math

import jax
import jax.numpy as jnp
from jax import lax
from jax.experimental import pallas as pl
from jax.experimental.pallas import tpu as pltpu

F32 = jnp.float32
BF16 = jnp.bfloat16

HALF_STEP = 0.5
NORM_EPS = 1e-6
CHUNK = 128
N_BRANCH = 3
A_HEADS = 4
A_QK = 128
A_V = 256
B_HEADS = 16
B_HEADDIM = 64
B_INNER = B_HEADS * B_HEADDIM
B_STATE = 128
B_GROUPS = 2
CONV_K = 5
C_HEADS = 8
C_Q_RANK = 512
C_KV_RANK = 512
C_NOPE = 128
C_ROPE = 64
C_V = 128
C_QK_PAD = 256
C_V_PAD = 256
ROPE_THETA = 10000.0
BRANCH_W = 1024
A_QK_W = A_HEADS * A_QK
A_V_W = A_HEADS * A_V
B_BC_W = 2 * B_GROUPS * B_STATE
N_GATE_ROWS = 2 * A_HEADS + 2 * A_HEADS + 2 * B_HEADS
GATE_LANES = 128
DT_COL0 = 4 * A_HEADS
SCAN_CHUNKS_PER_STEP = 4
VMEM_LIMIT = 56 * 1024 * 1024
HI = lax.Precision.HIGHEST
NT_DIMS = (((1,), (1,)), ((), ()))
TN_DIMS = (((0,), (0,)), ((), ()))


def _cparams(*sem):
    return pltpu.CompilerParams(dimension_semantics=sem, vmem_limit_bytes=VMEM_LIMIT)


def _tile(n, pref):
    t = min(n, pref)
    while n % t:
        t //= 2
    return t


def _softplus(x):
    return jnp.maximum(x, 0.0) + jnp.log1p(jnp.exp(-jnp.abs(x)))


def _sigmoid(x):
    return 1.0 / (1.0 + jnp.exp(-x))


def _rms(x, g):
    return x * lax.rsqrt(jnp.mean(x * x, axis=-1, keepdims=True) + NORM_EPS) * g


def _layout(d_model):
    off = {}
    o = 0
    for name, w in (("a_v", A_V_W), ("a_o", A_V_W), ("b_z", B_INNER), ("b_x", B_INNER),
                    ("gate", N_BRANCH * d_model), ("a_q", A_QK_W), ("a_k", A_QK_W),
                    ("b_bc", B_BC_W), ("c_q", C_Q_RANK), ("c_kv", C_KV_RANK), ("c_kr", 2 * C_ROPE)):
        off[name] = o
        o += w
    return off, o


def _ffn_kernel(x_ref, g_ref, w1_ref, w3_ref, w2_ref, o_ref, xn_ref):
    f = pl.program_id(1)

    @pl.when(f == 0)
    def _():
        xn_ref[...] = _rms(x_ref[...], g_ref[...]).astype(BF16)
        o_ref[...] = jnp.zeros_like(o_ref)

    xn = xn_ref[...]
    a = jnp.dot(xn, w1_ref[...], preferred_element_type=F32)
    b = jnp.dot(xn, w3_ref[...], preferred_element_type=F32)
    hid = (a * _sigmoid(a) * b).astype(BF16)
    o_ref[...] += jnp.dot(hid, w2_ref[...], preferred_element_type=F32)

    @pl.when(f == pl.num_programs(1) - 1)
    def _():
        o_ref[...] = x_ref[...] + HALF_STEP * o_ref[...]


def _ffn(h, g, w13, w2, li):
    t, d = h.shape
    d_ff = w2.shape[1]
    tm = _tile(t, 1024)
    tf = _tile(d_ff, 512)
    nf = d_ff // tf
    return pl.pallas_call(
        _ffn_kernel,
        out_shape=jax.ShapeDtypeStruct((t, d), F32),
        grid=(t // tm, nf),
        in_specs=[pl.BlockSpec((tm, d), lambda i, f: (i, 0)),
                  pl.BlockSpec((None, 1, d), lambda i, f: (li, 0, 0)),
                  pl.BlockSpec((None, d, tf), lambda i, f: (li, 0, f)),
                  pl.BlockSpec((None, d, tf), lambda i, f: (li, 0, nf + f)),
                  pl.BlockSpec((None, tf, d), lambda i, f: (li, f, 0))],
        out_specs=pl.BlockSpec((tm, d), lambda i, f: (i, 0)),
        scratch_shapes=[pltpu.VMEM((tm, d), BF16)],
        compiler_params=_cparams("parallel", "arbitrary"),
        name="ffn",
    )(h, g, w13, w13, w2)


def _gate_act(z, idx):
    return jnp.where(idx < 2 * A_HEADS, z,
                     jnp.where(idx < 4 * A_HEADS, -_softplus(-z), _softplus(z)))


def _proj_kernel(x_ref, g_ref, w_ref, wg_ref, wgt_ref, bc_ref, br_ref, o_ref, gc_ref, gr_ref, xn_ref):
    @pl.when(pl.program_id(1) == 0)
    def _():
        xn = _rms(x_ref[...], g_ref[...]).astype(BF16)
        xn_ref[...] = xn
        zc = jnp.dot(xn, wg_ref[...], preferred_element_type=F32) + bc_ref[...]
        gc_ref[...] = _gate_act(zc, lax.broadcasted_iota(jnp.int32, zc.shape, 1))
        zr = lax.dot_general(wgt_ref[...], xn, NT_DIMS, preferred_element_type=F32) + br_ref[...]
        gr_ref[...] = _gate_act(zr, lax.broadcasted_iota(jnp.int32, zr.shape, 0))

    o_ref[...] = jnp.dot(xn_ref[...], w_ref[...], preferred_element_type=F32).astype(o_ref.dtype)


def _proj(h, g, w_main, wg, wgt, bias_c, bias_r, li):
    t, d = h.shape
    n = w_main.shape[2]
    tm = _tile(t, 1024)
    tn = _tile(n, 1024)
    return pl.pallas_call(
        _proj_kernel,
        out_shape=(jax.ShapeDtypeStruct((t, n), BF16),
                   jax.ShapeDtypeStruct((t, GATE_LANES), F32),
                   jax.ShapeDtypeStruct((N_GATE_ROWS, t), F32)),
        grid=(t // tm, n // tn),
        in_specs=[pl.BlockSpec((tm, d), lambda i, j: (i, 0)),
                  pl.BlockSpec((None, 1, d), lambda i, j: (li, 0, 0)),
                  pl.BlockSpec((None, d, tn), lambda i, j: (li, 0, j)),
                  pl.BlockSpec((None, d, GATE_LANES), lambda i, j: (li, 0, 0)),
                  pl.BlockSpec((None, N_GATE_ROWS, d), lambda i, j: (li, 0, 0)),
                  pl.BlockSpec((None, 1, GATE_LANES), lambda i, j: (li, 0, 0)),
                  pl.BlockSpec((None, N_GATE_ROWS, 1), lambda i, j: (li, 0, 0))],
        out_specs=(pl.BlockSpec((tm, tn), lambda i, j: (i, j)),
                   pl.BlockSpec((tm, GATE_LANES), lambda i, j: (i, 0)),
                   pl.BlockSpec((N_GATE_ROWS, tm), lambda i, j: (0, i))),
        scratch_shapes=[pltpu.VMEM((tm, d), BF16)],
        compiler_params=_cparams("parallel", "arbitrary"),
        name="in_proj",
    )(h, g, w_main, wg, wgt, bias_c, bias_r)


def _chunk_mask(reverse):
    ti = lax.broadcasted_iota(jnp.int32, (CHUNK, CHUNK), 0)
    si = lax.broadcasted_iota(jnp.int32, (CHUNK, CHUNK), 1)
    return (si >= ti) if reverse else (si <= ti)


def _mlstm_kernel(*refs, reverse, final):
    if final:
        (q_ref, k_ref, v_ref, gc_ref, gr_ref, o_ref, hf_ref, ng_ref,
         out_ref, c_ref, n_ref, m_ref) = refs
    else:
        q_ref, k_ref, v_ref, gc_ref, gr_ref, out_ref, c_ref, n_ref, m_ref = refs

    @pl.when(pl.program_id(1) == 0)
    def _():
        c_ref[...] = jnp.zeros_like(c_ref)
        n_ref[...] = jnp.zeros_like(n_ref)
        m_ref[...] = jnp.zeros_like(m_ref)

    allowed = _chunk_mask(reverse)
    amat = allowed.astype(F32)
    scale = A_QK ** -0.5
    d = 1 if reverse else 0
    n_sub = q_ref.shape[1] // CHUNK
    m_st = [m_ref[hd][:, 0:1] for hd in range(A_HEADS)]
    c_st = [c_ref[hd] for hd in range(A_HEADS)]
    n_st = [n_ref[hd] for hd in range(A_HEADS)]
    for cc in (reversed(range(n_sub)) if reverse else range(n_sub)):
        rows = slice(cc * CHUNK, (cc + 1) * CHUNK)
        gc = gc_ref[0, rows, :]
        gr = gr_ref[:, rows]
        cum_c = jnp.dot(amat, gc, precision=HI, preferred_element_type=F32)
        cum_r = lax.dot_general(gr, amat, NT_DIMS, precision=HI, preferred_element_type=F32)
        for hd in range(A_HEADS):
            ii = d * A_HEADS + hd
            fi = 2 * A_HEADS + d * A_HEADS + hd
            li_c = gc[:, ii:ii + 1]
            li_r = gr[ii:ii + 1, :]
            b_c = cum_c[:, fi:fi + 1]
            b_r = cum_r[fi:fi + 1, :]
            g_tot = jnp.sum(gc[:, fi:fi + 1], axis=0, keepdims=True)
            m0, c0, n0 = m_st[hd], c_st[hd], n_st[hd]
            qh = q_ref[0, rows, hd * A_QK:(hd + 1) * A_QK]
            kh = k_ref[0, rows, hd * A_QK:(hd + 1) * A_QK]
            vh = v_ref[0, rows, hd * A_V:(hd + 1) * A_V]
            kf = kh.astype(F32)
            w_state = g_tot - b_c + li_c
            m_loc = jnp.max(w_state, axis=0, keepdims=True)
            es = jnp.exp(w_state - m_loc) * scale
            ke = kf * es
            c_loc = lax.dot_general(ke.astype(BF16), vh, TN_DIMS, preferred_element_type=F32)
            n_loc = jnp.sum(ke, axis=0, keepdims=True)
            r_sel = jnp.where(allowed, li_r - b_r, -jnp.inf)
            mm = jnp.maximum(m0, jnp.max(r_sel, axis=1, keepdims=True))
            m_t = b_c + mm
            qk = lax.dot_general(qh, kh, NT_DIMS, preferred_element_type=F32)
            s_qk = qk * (jnp.exp(r_sel - mm) * scale)
            e_inter = jnp.exp(m0 - mm)
            num = (jnp.dot(s_qk.astype(BF16), vh, preferred_element_type=F32)
                   + e_inter * jnp.dot(qh, c0.astype(BF16), preferred_element_type=F32))
            den = (jnp.sum(s_qk, axis=1, keepdims=True)
                   + e_inter * jnp.sum(qh.astype(F32) * n0, axis=1, keepdims=True))
            hout = num / jnp.maximum(jnp.abs(den), jnp.exp(-m_t))
            m_new = jnp.maximum(g_tot + m0, m_loc)
            a_prev = jnp.exp(g_tot + m0 - m_new)
            a_loc = jnp.exp(m_loc - m_new)
            c_st[hd] = a_prev * c0 + a_loc * c_loc
            n_st[hd] = a_prev * n0 + a_loc * n_loc
            m_st[hd] = m_new
            sl = slice(hd * A_V, (hd + 1) * A_V)
            if final:
                htot = hout + hf_ref[0, rows, sl]
                hn = htot * lax.rsqrt(jnp.mean(htot * htot, axis=-1, keepdims=True) + NORM_EPS)
                out_ref[0, rows, sl] = (_sigmoid(o_ref[0, rows, sl].astype(F32)) * hn
                                        * ng_ref[:, sl]).astype(out_ref.dtype)
            else:
                out_ref[0, rows, sl] = hout
    for hd in range(A_HEADS):
        c_ref[hd] = c_st[hd]
        n_ref[hd] = n_st[hd]
        m_ref[hd] = jnp.broadcast_to(m_st[hd], (1, GATE_LANES))


def _mlstm_dir(main3, gc3, gr, off, reverse, h_fwd=None, norm_g=None, li=0):
    bsz, seq, _ = main3.shape
    rows = SCAN_CHUNKS_PER_STEP * CHUNK
    nc = seq // rows
    final = h_fwd is not None

    def cidx(c):
        return (nc - 1 - c) if reverse else c

    def col(name, width):
        blk = off[name] // width
        return pl.BlockSpec((1, rows, width), lambda b, c: (b, cidx(c), blk))

    in_specs = [col("a_q", A_QK_W), col("a_k", A_QK_W), col("a_v", A_V_W),
                pl.BlockSpec((1, rows, GATE_LANES), lambda b, c: (b, cidx(c), 0)),
                pl.BlockSpec((N_GATE_ROWS, rows), lambda b, c: (0, b * nc + cidx(c)))]
    args = [main3, main3, main3, gc3, gr]
    if final:
        in_specs += [col("a_o", A_V_W),
                     pl.BlockSpec((1, rows, A_V_W), lambda b, c: (b, cidx(c), 0)),
                     pl.BlockSpec((None, 1, A_V_W), lambda b, c: (li, 0, 0))]
        args += [main3, h_fwd, norm_g]
    return pl.pallas_call(
        functools.partial(_mlstm_kernel, reverse=reverse, final=final),
        out_shape=jax.ShapeDtypeStruct((bsz, seq, A_V_W), BF16 if final else F32),
        grid=(bsz, nc),
        in_specs=in_specs,
        out_specs=pl.BlockSpec((1, rows, A_V_W), lambda b, c: (b, cidx(c), 0)),
        scratch_shapes=[pltpu.VMEM((A_HEADS, A_QK, A_V), F32),
                        pltpu.VMEM((A_HEADS, 1, A_QK), F32),
                        pltpu.VMEM((A_HEADS, 1, GATE_LANES), F32)],
        compiler_params=_cparams("parallel", "arbitrary"),
        name="mlstm_bwd" if reverse else "mlstm_fwd",
    )(*args)


def _conv_kernel(xc_ref, xp_ref, xn_ref, bc_ref, bp_ref, bn_ref, w_ref, b_ref, ox_ref, obc_ref, sx_ref, sbc_ref):
    i = pl.program_id(1)
    first = i == 0
    last = i == pl.num_programs(1) - 1
    pad = (CONV_K - 1) // 2

    def run(cur_ref, prev_ref, next_ref, scr_ref, out_ref, c0):
        tb, w = cur_ref.shape[1], cur_ref.shape[2]
        scr_ref[0:8, :] = jnp.where(first, 0.0, prev_ref[0].astype(F32))
        scr_ref[8:8 + tb, :] = cur_ref[0].astype(F32)
        scr_ref[8 + tb:16 + tb, :] = jnp.where(last, 0.0, next_ref[0].astype(F32))
        acc = jnp.zeros((tb, w), F32) + b_ref[:, c0:c0 + w]
        for j in range(CONV_K):
            acc = acc + scr_ref[8 - pad + j:8 - pad + j + tb, :] * w_ref[j:j + 1, c0:c0 + w]
        out_ref[0] = (acc * _sigmoid(acc)).astype(out_ref.dtype)

    run(xc_ref, xp_ref, xn_ref, sx_ref, ox_ref, 0)
    run(bc_ref, bp_ref, bn_ref, sbc_ref, obc_ref, B_INNER)


def _conv(main3, conv_w, conv_b, off, li):
    bsz, seq, _ = main3.shape
    tb = _tile(seq, 512)
    nb = seq // tb
    r8 = tb // 8
    last8 = seq // 8 - 1

    def trio(name, width):
        blk = off[name] // width
        return [pl.BlockSpec((1, tb, width), lambda b, i: (b, i, blk)),
                pl.BlockSpec((1, 8, width), lambda b, i: (b, jnp.maximum(i * r8 - 1, 0), blk)),
                pl.BlockSpec((1, 8, width), lambda b, i: (b, jnp.minimum((i + 1) * r8, last8), blk))]

    wtot = B_INNER + B_BC_W
    return pl.pallas_call(
        _conv_kernel,
        out_shape=(jax.ShapeDtypeStruct((bsz, seq, B_INNER), BF16),
                   jax.ShapeDtypeStruct((bsz, seq, B_BC_W), BF16)),
        grid=(bsz, nb),
        in_specs=trio("b_x", B_INNER) + trio("b_bc", B_BC_W) + [
            pl.BlockSpec((None, CONV_K, wtot), lambda b, i: (li, 0, 0)),
            pl.BlockSpec((None, 1, wtot), lambda b, i: (li, 0, 0))],
        out_specs=(pl.BlockSpec((1, tb, B_INNER), lambda b, i: (b, i, 0)),
                   pl.BlockSpec((1, tb, B_BC_W), lambda b, i: (b, i, 0))),
        scratch_shapes=[pltpu.VMEM((tb + 16, B_INNER), F32), pltpu.VMEM((tb + 16, B_BC_W), F32)],
        compiler_params=_cparams("parallel", "parallel"),
        name="dwconv_silu",
    )(main3, main3, main3, main3, main3, main3, conv_w, conv_b)


def _ssd_kernel(*refs, reverse, final):
    if final:
        (xs_ref, bc_ref, gc_ref, gr_ref, alr_ref, alc_ref, z_ref, yf_ref, dsk_ref, ng_ref,
         out_ref, s_ref, y_ref) = refs
    else:
        xs_ref, bc_ref, gc_ref, gr_ref, alr_ref, alc_ref, out_ref, s_ref, y_ref = refs

    @pl.when(pl.program_id(1) == 0)
    def _():
        s_ref[...] = jnp.zeros_like(s_ref)

    d = 1 if reverse else 0
    col0 = DT_COL0 + d * B_HEADS
    allowed = _chunk_mask(reverse)
    amat = allowed.astype(F32)
    lane = lax.broadcasted_iota(jnp.int32, (1, GATE_LANES), 1)
    a_row = jnp.where((lane >= col0) & (lane < col0 + B_HEADS), -jnp.exp(alr_ref[...]), 0.0)
    row = lax.broadcasted_iota(jnp.int32, (N_GATE_ROWS, 1), 0)
    a_col = jnp.where((row >= col0) & (row < col0 + B_HEADS), -jnp.exp(alc_ref[...]), 0.0)
    er = lax.broadcasted_iota(jnp.int32, (GATE_LANES, B_INNER), 0)
    ec = lax.broadcasted_iota(jnp.int32, (GATE_LANES, B_INNER), 1)
    expand = (er == col0 + (ec >> (B_HEADDIM.bit_length() - 1))).astype(F32)
    expand_b = expand.astype(BF16)

    def spread(t):
        return jnp.dot(t.astype(BF16), expand_b, preferred_element_type=F32)

    gw = B_INNER // B_GROUPS
    hpg = B_HEADS // B_GROUPS
    n_sub = xs_ref.shape[1] // CHUNK
    s_st = [s_ref[g] for g in range(B_GROUPS)]
    for cc in (reversed(range(n_sub)) if reverse else range(n_sub)):
        rows = slice(cc * CHUNK, (cc + 1) * CHUNK)
        gc = gc_ref[0, rows, :]
        gr = gr_ref[:, rows]
        dta_c = gc * a_row
        dta_r = gr * a_col
        acs_c = jnp.dot(amat, dta_c, precision=HI, preferred_element_type=F32)
        acs_r = lax.dot_general(dta_r, amat, NT_DIMS, precision=HI, preferred_element_type=F32)
        tot = jnp.sum(dta_c, axis=0, keepdims=True)
        dt_x = spread(gc)
        to_end_x = spread(jnp.exp(tot - acs_c))
        from_start_x = spread(jnp.exp(acs_c))
        chunk_decay_x = jnp.dot(jnp.broadcast_to(jnp.exp(tot), (8, GATE_LANES)), expand, precision=HI,
                                preferred_element_type=F32)[0:1]
        xs = xs_ref[0, rows, :].astype(F32)
        xdt = xs * dt_x
        xdt_b = xdt.astype(BF16)
        xend_b = (xdt * to_end_x).astype(BF16)
        for g in range(B_GROUPS):
            bm = bc_ref[0, rows, g * B_STATE:(g + 1) * B_STATE]
            cm = bc_ref[0, rows, (B_GROUPS + g) * B_STATE:(B_GROUPS + g + 1) * B_STATE]
            gs = slice(g * gw, (g + 1) * gw)
            s0 = s_st[g]
            y_ref[rows, gs] = jnp.dot(cm, s0.astype(BF16), preferred_element_type=F32) * from_start_x[:, gs]
            states = lax.dot_general(bm, xend_b[:, gs], TN_DIMS, preferred_element_type=F32)
            s_st[g] = chunk_decay_x[:, gs] * s0 + states
            cb = lax.dot_general(cm, bm, NT_DIMS, preferred_element_type=F32)
            for hh in range(hpg):
                ci = col0 + g * hpg + hh
                seg = acs_c[:, ci:ci + 1] - acs_r[ci:ci + 1, :]
                dec = jnp.exp(jnp.where(allowed, seg, -jnp.inf))
                hs = slice((g * hpg + hh) * B_HEADDIM, (g * hpg + hh + 1) * B_HEADDIM)
                y_ref[rows, hs] += jnp.dot((cb * dec).astype(BF16), xdt_b[:, hs], preferred_element_type=F32)
    for g in range(B_GROUPS):
        s_ref[g] = s_st[g]
    if final:
        y = y_ref[...] + yf_ref[0] + dsk_ref[...] * xs_ref[0].astype(F32)
        zf = z_ref[0].astype(F32)
        out_ref[0] = _rms(y * (zf * _sigmoid(zf)), ng_ref[...]).astype(out_ref.dtype)
    else:
        out_ref[0] = y_ref[...]


def _ssd_dir(xs3, bc3, gc3, gr, alog_r, alog_c, reverse, li, main3=None, off=None, y_fwd=None, d_skip=None,
             norm_g=None):
    bsz, seq, _ = xs3.shape
    rows = SCAN_CHUNKS_PER_STEP * CHUNK
    nc = seq // rows
    final = y_fwd is not None

    def cidx(c):
        return (nc - 1 - c) if reverse else c

    def seq_spec(width, blk=0):
        return pl.BlockSpec((1, rows, width), lambda b, c: (b, cidx(c), blk))

    def const_spec(shape):
        return pl.BlockSpec((None,) + shape, lambda b, c: (li, 0, 0))

    in_specs = [seq_spec(B_INNER), seq_spec(B_BC_W), seq_spec(GATE_LANES),
                pl.BlockSpec((N_GATE_ROWS, rows), lambda b, c: (0, b * nc + cidx(c))),
                const_spec((1, GATE_LANES)), const_spec((N_GATE_ROWS, 1))]
    args = [xs3, bc3, gc3, gr, alog_r, alog_c]
    if final:
        in_specs += [seq_spec(B_INNER, off["b_z"] // B_INNER), seq_spec(B_INNER),
                     const_spec((1, B_INNER)), const_spec((1, B_INNER))]
        args += [main3, y_fwd, d_skip, norm_g]
    return pl.pallas_call(
        functools.partial(_ssd_kernel, reverse=reverse, final=final),
        out_shape=jax.ShapeDtypeStruct((bsz, seq, B_INNER), BF16 if final else F32),
        grid=(bsz, nc),
        in_specs=in_specs,
        out_specs=seq_spec(B_INNER),
        scratch_shapes=[pltpu.VMEM((B_GROUPS, B_STATE, B_INNER // B_GROUPS), F32),
                        pltpu.VMEM((rows, B_INNER), F32)],
        compiler_params=_cparams("parallel", "arbitrary"),
        name="ssd_bwd" if reverse else "ssd_fwd",
    )(*args)


def _rope_kernel(pos_ref, tab_ref):
    half = C_ROPE // 2
    lane = lax.broadcasted_iota(jnp.int32, (1, 4 * half), 1)
    fr = (lane & (half - 1)).astype(F32)
    inv_freq = jnp.exp(fr * (-math.log(ROPE_THETA) / half))
    ang = pos_ref[...].astype(F32) * inv_freq
    sn = jnp.sin(ang)
    tab_ref[...] = jnp.where(lane < 2 * half, jnp.cos(ang), jnp.where(lane < 3 * half, -sn, sn))


def _rope_table(pos_col):
    t = pos_col.shape[0]
    tm = _tile(t, 1024)
    return pl.pallas_call(
        _rope_kernel,
        out_shape=jax.ShapeDtypeStruct((t, 2 * C_ROPE), F32),
        grid=(t // tm,),
        in_specs=[pl.BlockSpec((tm, 1), lambda i: (i, 0))],
        out_specs=pl.BlockSpec((tm, 2 * C_ROPE), lambda i: (i, 0)),
        compiler_params=_cparams("parallel"),
        name="rope_table",
    )(pos_col)


def _mla_prep_kernel(cq_ref, ckv_ref, ckr_ref, tab_ref, qg_ref, kvg_ref, wq_ref, wkv_ref, q_ref, k_ref, v_ref):
    tab = tab_ref[0]
    lane = lax.broadcasted_iota(jnp.int32, tab.shape, 1)

    def rope(t, zero_upper):
        p = t * tab
        r = p + pltpu.roll(p, C_ROPE, axis=1)
        return jnp.where(lane < C_ROPE, r, 0.0) if zero_upper else r

    cqn = _rms(cq_ref[0].astype(F32), qg_ref[...]).astype(BF16)
    ckvn = _rms(ckv_ref[0].astype(F32), kvg_ref[...]).astype(BF16)
    q_all = jnp.dot(cqn, wq_ref[...], preferred_element_type=F32)
    kv_all = jnp.dot(ckvn, wkv_ref[...], preferred_element_type=F32)
    k_rot = rope(ckr_ref[0].astype(F32), False).astype(BF16)
    qscale = (C_NOPE + C_ROPE) ** -0.5 * math.log2(math.e)
    ones_col = (lane == 0).astype(BF16)
    for hd in range(C_HEADS):
        q0 = hd * C_QK_PAD
        q_ref[0, hd, :, 0:C_NOPE] = (q_all[:, q0:q0 + C_NOPE] * qscale).astype(BF16)
        q_ref[0, hd, :, C_NOPE:C_QK_PAD] = (rope(q_all[:, q0 + C_NOPE:q0 + C_QK_PAD], True) * qscale).astype(BF16)
        k_ref[0, hd, :, 0:C_NOPE] = kv_all[:, hd * C_NOPE:(hd + 1) * C_NOPE].astype(BF16)
        k_ref[0, hd, :, C_NOPE:C_QK_PAD] = k_rot
        v0 = C_HEADS * C_NOPE + hd * C_V
        v_ref[0, hd, :, 0:C_V] = kv_all[:, v0:v0 + C_V].astype(BF16)
        v_ref[0, hd, :, C_V:C_V_PAD] = ones_col


def _mla_prep(main3, tab3, q_g, kv_g, wq, wkv, off, li):
    bsz, seq, _ = main3.shape
    tm = _tile(seq, 512)

    def col(name, width):
        blk = off[name] // width
        return pl.BlockSpec((1, tm, width), lambda b, i: (b, i, blk))

    def const_spec(shape):
        return pl.BlockSpec((None,) + tuple(shape), lambda b, i: (li, 0, 0))

    def head_spec(width):
        return pl.BlockSpec((1, C_HEADS, tm, width), lambda b, i: (b, 0, i, 0))

    return pl.pallas_call(
        _mla_prep_kernel,
        out_shape=(jax.ShapeDtypeStruct((bsz, C_HEADS, seq, C_QK_PAD), BF16),
                   jax.ShapeDtypeStruct((bsz, C_HEADS, seq, C_QK_PAD), BF16),
                   jax.ShapeDtypeStruct((bsz, C_HEADS, seq, C_V_PAD), BF16)),
        grid=(bsz, seq // tm),
        in_specs=[col("c_q", C_Q_RANK), col("c_kv", C_KV_RANK), col("c_kr", 2 * C_ROPE),
                  pl.BlockSpec((1, tm, 2 * C_ROPE), lambda b, i: (b, i, 0)),
                  const_spec((1, C_Q_RANK)), const_spec((1, C_KV_RANK)),
                  const_spec(wq.shape[1:]), const_spec(wkv.shape[1:])],
        out_specs=(head_spec(C_QK_PAD), head_spec(C_QK_PAD), head_spec(C_V_PAD)),
        compiler_params=_cparams("parallel", "parallel"),
        name="mla_prep",
    )(main3, main3, main3, tab3, q_g, kv_g, wq, wkv)


def _attn_kernel(q_ref, k_ref, v_ref, o_ref, *, tk):
    q = q_ref[0, 0]
    nk = k_ref.shape[2] // tk
    m = None
    acc = None
    for j in range(nk):
        kb = k_ref[0, 0, j * tk:(j + 1) * tk, :]
        vb = v_ref[0, 0, j * tk:(j + 1) * tk, :]
        s = lax.dot_general(q, kb, NT_DIMS, preferred_element_type=F32)
        m_blk = jnp.max(s, axis=1, keepdims=True)
        m_new = m_blk if m is None else jnp.maximum(m, m_blk)
        pv = jnp.dot(jnp.exp2(s - m_new).astype(BF16), vb, preferred_element_type=F32)
        acc = pv if acc is None else jnp.exp2(m - m_new) * acc + pv
        m = m_new
    o_ref[0] = (acc[:, 0:C_V] / acc[:, C_V:C_V + 1]).astype(o_ref.dtype)


def _attention(q4, k4, v4):
    bsz, nh, seq, _ = q4.shape
    tq = _tile(seq, 1024)
    tk = _tile(seq, 512)
    return pl.pallas_call(
        functools.partial(_attn_kernel, tk=tk),
        out_shape=jax.ShapeDtypeStruct((bsz, seq, nh * C_V), BF16),
        grid=(bsz, nh, seq // tq),
        in_specs=[pl.BlockSpec((1, 1, tq, C_QK_PAD), lambda b, h, i: (b, h, i, 0)),
                  pl.BlockSpec((1, 1, seq, C_QK_PAD), lambda b, h, i: (b, h, 0, 0)),
                  pl.BlockSpec((1, 1, seq, C_V_PAD), lambda b, h, i: (b, h, 0, 0))],
        out_specs=pl.BlockSpec((1, tq, C_V), lambda b, h, i: (b, i, h)),
        compiler_params=_cparams("parallel", "parallel", "parallel"),
        name="mla_attention",
    )(q4, k4, v4)


def _merge_kernel(ya_ref, yb_ref, yc_ref, wb_ref, g0_ref, g1_ref, g2_ref, o_ref):
    acc = None
    for y_ref, g_ref, k in ((ya_ref, g0_ref, 0), (yb_ref, g1_ref, 1), (yc_ref, g2_ref, 2)):
        t = _sigmoid(g_ref[...].astype(F32)) * jnp.dot(y_ref[...], wb_ref[k], preferred_element_type=F32)
        acc = t if acc is None else acc + t
    o_ref[...] = acc.astype(o_ref.dtype)


def _merge(ya, yb, yc, wb, main, off, d_model, li):
    t = ya.shape[0]
    tm = _tile(t, 1024)
    tn = _tile(d_model, 512)
    gblk = off["gate"] // tn
    per = d_model // tn

    def gate_spec(k):
        return pl.BlockSpec((tm, tn), lambda i, j: (i, gblk + k * per + j))

    y_spec = pl.BlockSpec((tm, BRANCH_W), lambda i, j: (i, 0))
    return pl.pallas_call(
        _merge_kernel,
        out_shape=jax.ShapeDtypeStruct((t, d_model), BF16),
        grid=(t // tm, per),
        in_specs=[y_spec, y_spec, y_spec,
                  pl.BlockSpec((None, N_BRANCH, BRANCH_W, tn), lambda i, j: (li, 0, 0, j)),
                  gate_spec(0), gate_spec(1), gate_spec(2)],
        out_specs=pl.BlockSpec((tm, tn), lambda i, j: (i, j)),
        compiler_params=_cparams("parallel", "parallel"),
        name="branch_merge",
    )(ya, yb, yc, wb, main, main, main)


def _mm_res_kernel(a_ref, w_ref, r_ref, o_ref):
    o_ref[...] = r_ref[...] + jnp.dot(a_ref[...], w_ref[...], preferred_element_type=F32)


def _mm_res(a, w, res, li):
    t, k = a.shape
    n = w.shape[2]
    tm = _tile(t, 1024)
    tn = _tile(n, 1024)
    return pl.pallas_call(
        _mm_res_kernel,
        out_shape=jax.ShapeDtypeStruct((t, n), F32),
        grid=(t // tm, n // tn),
        in_specs=[pl.BlockSpec((tm, k), lambda i, j: (i, 0)),
                  pl.BlockSpec((None, k, tn), lambda i, j: (li, 0, j)),
                  pl.BlockSpec((tm, tn), lambda i, j: (i, j))],
        out_specs=pl.BlockSpec((tm, tn), lambda i, j: (i, j)),
        compiler_params=_cparams("parallel", "parallel"),
        name="out_proj",
    )(a, w, res)


def _ple_kernel(x_ref, g_ref, wg_ref, p_ref, wp_ref, *rest, last):
    x = x_ref[...]
    xn = _rms(x, g_ref[...]).astype(BF16)
    gate = _sigmoid(jnp.dot(xn, wg_ref[...], preferred_element_type=F32))
    emb = jnp.dot(p_ref[...].astype(BF16), wp_ref[...], preferred_element_type=F32)
    h_new = x + gate * emb
    if last:
        fg_ref, o_ref = rest
        o_ref[...] = _rms(h_new, fg_ref[...])
    else:
        rest[0][...] = h_new


def _ple(h, g, wgate, p, wproj, li, final_g=None):
    t, d = h.shape
    pd = p.shape[2]
    tm = _tile(t, 512)
    last = final_g is not None
    in_specs = [pl.BlockSpec((tm, d), lambda i: (i, 0)),
                pl.BlockSpec((None, 1, d), lambda i: (li, 0, 0)),
                pl.BlockSpec((None, d, d), lambda i: (li, 0, 0)),
                pl.BlockSpec((None, tm, pd), lambda i: (li, i, 0)),
                pl.BlockSpec((None, pd, d), lambda i: (li, 0, 0))]
    args = [h, g, wgate, p, wproj]
    if last:
        in_specs.append(pl.BlockSpec((1, d), lambda i: (0, 0)))
        args.append(final_g)
    return pl.pallas_call(
        functools.partial(_ple_kernel, last=last),
        out_shape=jax.ShapeDtypeStruct((t, d), F32),
        grid=(t // tm,),
        in_specs=in_specs,
        out_specs=pl.BlockSpec((tm, d), lambda i: (i, 0)),
        compiler_params=_cparams("parallel"),
        name="ple",
    )(*args)


def _prep_w_in(w_in, d_model):
    sizes = (A_QK_W, A_QK_W, A_V_W, A_V_W, 2 * A_HEADS, 2 * A_HEADS, B_INNER, B_INNER + B_BC_W,
             2 * B_HEADS, C_Q_RANK, C_KV_RANK, C_ROPE, N_BRANCH * d_model)
    w_in = w_in.astype(BF16)
    cuts, o = [], 0
    for s in sizes:
        cuts.append((o, o + s))
        o += s
    (a_q, a_k, a_v, a_o, a_ig, a_fg, b_z, b_xbc, b_dt, c_q, c_kv, c_kr, gate) = [w_in[..., lo:hi] for lo, hi in cuts]
    half = C_ROPE // 2
    kr_swapped = jnp.concatenate([c_kr[..., half:], c_kr[..., :half]], axis=-1)
    pieces = {"a_v": a_v, "a_o": a_o, "b_z": b_z, "b_x": b_xbc[..., :B_INNER], "gate": gate, "a_q": a_q,
              "a_k": a_k, "b_bc": b_xbc[..., B_INNER:], "c_q": c_q, "c_kv": c_kv,
              "c_kr": jnp.concatenate([c_kr, kr_swapped], axis=-1)}
    off, n1 = _layout(d_model)
    order = sorted(off, key=off.get)
    n_pad = -n1 % 1024
    cols = [pieces[k] for k in order]
    if n_pad:
        cols.append(jnp.zeros(w_in.shape[:-1] + (n_pad,), w_in.dtype))
    w_main = jnp.concatenate(cols, axis=-1).astype(BF16)
    gates = jnp.concatenate([a_ig, a_fg, b_dt], axis=-1)
    wg = jnp.concatenate([gates, jnp.zeros(w_in.shape[:-1] + (GATE_LANES - N_GATE_ROWS,), w_in.dtype)],
                         axis=-1).astype(BF16)
    wgt = jnp.swapaxes(gates, -1, -2).astype(BF16)
    return w_main, wg, wgt, off


def _prep_mla(w_uq, w_ukv):
    nl, rq, _ = w_uq.shape
    half = C_ROPE // 2
    wq = w_uq.reshape(nl, rq, C_HEADS, C_NOPE + C_ROPE)
    t1 = wq[..., C_NOPE:C_NOPE + half]
    t2 = wq[..., C_NOPE + half:]
    wq = jnp.concatenate([wq[..., :C_NOPE], t1, t2, t2, t1], axis=-1).reshape(nl, rq, C_HEADS * C_QK_PAD)
    wkv = w_ukv.reshape(nl, w_ukv.shape[1], C_HEADS, C_NOPE + C_V)
    wkv = jnp.concatenate([wkv[..., :C_NOPE].reshape(nl, -1, C_HEADS * C_NOPE),
                           wkv[..., C_NOPE:].reshape(nl, -1, C_HEADS * C_V)], axis=-1)
    return wq.astype(BF16), wkv.astype(BF16)


def kernel(x, p, positions, ffn1_norm, ffn1_w13, ffn1_w2, mix_norm, w_in, mlstm_b_igate, mlstm_b_fgate, mlstm_norm, conv_w, conv_b, ssm_a_log, ssm_dt_bias, ssm_d, ssm_norm, mla_q_norm, mla_kv_norm, mla_w_uq, mla_w_ukv, w_branch, w_out, ffn2_norm, ffn2_w13, ffn2_w2, ple_norm, w_ple_gate, w_ple_proj, final_norm):
    bsz, seq, d = x.shape
    depth = w_in.shape[0]
    t = bsz * seq
    nl = depth

    w_main, wg, wgt, off = _prep_w_in(w_in, d)
    wq, wkv = _prep_mla(mla_w_uq, mla_w_ukv)
    ffn1_w13b, ffn1_w2b = ffn1_w13.astype(BF16), ffn1_w2.astype(BF16)
    ffn2_w13b, ffn2_w2b = ffn2_w13.astype(BF16), ffn2_w2.astype(BF16)
    w_branch_b, w_out_b = w_branch.astype(BF16), w_out.astype(BF16)
    w_pg_b, w_pp_b = w_ple_gate.astype(BF16), w_ple_proj.astype(BF16)
    gate_bias = jnp.concatenate([mlstm_b_igate.reshape(nl, -1), mlstm_b_fgate.reshape(nl, -1),
                                 ssm_dt_bias.reshape(nl, -1)], axis=-1).astype(F32)
    bias_c = jnp.pad(gate_bias, ((0, 0), (0, GATE_LANES - N_GATE_ROWS)))[:, None, :]
    bias_r = gate_bias[:, :, None]
    alog = ssm_a_log.reshape(nl, -1).astype(F32)
    alog_r = jnp.pad(alog, ((0, 0), (DT_COL0, GATE_LANES - N_GATE_ROWS)))[:, None, :]
    alog_c = jnp.pad(alog, ((0, 0), (DT_COL0, 0)))[:, :, None]
    d_skip = jnp.repeat(ssm_d.astype(F32), B_HEADDIM, axis=-1)[:, None, :]
    row = lambda v: v[:, None, :]
    p3 = p.reshape(nl, t, -1)

    tab3 = _rope_table(positions.reshape(t, 1)).reshape(bsz, seq, 2 * C_ROPE)

    h = x.reshape(t, d)
    for i in range(depth):
        h = _ffn(h, row(ffn1_norm), ffn1_w13b, ffn1_w2b, i)
        main, gc, gr = _proj(h, row(mix_norm), w_main, wg, wgt, bias_c, bias_r, i)
        main3 = main.reshape(bsz, seq, -1)
        gc3 = gc.reshape(bsz, seq, GATE_LANES)
        hf = _mlstm_dir(main3, gc3, gr, off, False)
        ya = _mlstm_dir(main3, gc3, gr, off, True, h_fwd=hf, norm_g=row(mlstm_norm), li=i)
        xs3, bc3 = _conv(main3, conv_w, row(conv_b), off, i)
        yf = _ssd_dir(xs3, bc3, gc3, gr, alog_r, alog_c, False, i)
        yb = _ssd_dir(xs3, bc3, gc3, gr, alog_r, alog_c, True, i, main3=main3, off=off, y_fwd=yf,
                      d_skip=d_skip, norm_g=row(ssm_norm))
        q4, k4, v4 = _mla_prep(main3, tab3, row(mla_q_norm), row(mla_kv_norm), wq, wkv, off, i)
        yc = _attention(q4, k4, v4)
        merged = _merge(ya.reshape(t, -1), yb.reshape(t, -1), yc.reshape(t, -1), w_branch_b, main, off, d, i)
        h = _mm_res(merged, w_out_b, h, i)
        h = _ffn(h, row(ffn2_norm), ffn2_w13b, ffn2_w2b, i)
        h = _ple(h, row(ple_norm), w_pg_b, p3, w_pp_b, i, final_g=final_norm[None] if i == depth - 1 else None)
    return h.reshape(bsz, seq, d)
```

```python
import functools
import math

import jax
import jax.numpy as jnp
from jax import lax
from jax.experimental import pallas as pl
from jax.experimental.pallas import tpu as pltpu

F32 = jnp.float32
BF16 = jnp.bfloat16

HALF_STEP = 0.5
NORM_EPS = 1e-6
CHUNK = 128
N_BRANCH = 3
A_HEADS = 4
A_QK = 128
A_V = 256
B_HEADS = 16
B_HEADDIM = 64
B_INNER = B_HEADS * B_HEADDIM
B_STATE = 128
B_GROUPS = 2
CONV_K = 5
C_HEADS = 8
C_Q_RANK = 512
C_KV_RANK = 512
C_NOPE = 128
C_ROPE = 64
C_V = 128
C_QK_PAD = 256
C_V_PAD = 256
ROPE_THETA = 10000.0
BRANCH_W = 1024
A_QK_W = A_HEADS * A_QK
A_V_W = A_HEADS * A_V
B_BC_W = 2 * B_GROUPS * B_STATE
N_GATE_ROWS = 2 * A_HEADS + 2 * A_HEADS + 2 * B_HEADS
GATE_LANES = 128
DT_COL0 = 4 * A_HEADS
SCAN_CHUNKS_PER_STEP = 4
VMEM_LIMIT = 56 * 1024 * 1024
HI = lax.Precision.HIGHEST
NT_DIMS = (((1,), (1,)), ((), ()))
TN_DIMS = (((0,), (0,)), ((), ()))


def _cparams(*sem):
    return pltpu.CompilerParams(dimension_semantics=sem, vmem_limit_bytes=VMEM_LIMIT)


def _tile(n, pref):
    t = min(n, pref)
    while n % t:
        t //= 2
    return t


def _softplus(x):
    return jnp.maximum(x, 0.0) + jnp.log1p(jnp.exp(-jnp.abs(x)))


def _sigmoid(x):
    return 1.0 / (1.0 + jnp.exp(-x))


def _rms(x, g):
    return x * lax.rsqrt(jnp.mean(x * x, axis=-1, keepdims=True) + NORM_EPS) * g


def _layout(d_model):
    off = {}
    o = 0
    for name, w in (("a_v", A_V_W), ("a_o", A_V_W), ("b_z", B_INNER), ("b_x", B_INNER),
                    ("gate", N_BRANCH * d_model), ("a_q", A_QK_W), ("a_k", A_QK_W),
                    ("b_bc", B_BC_W), ("c_q", C_Q_RANK), ("c_kv", C_KV_RANK), ("c_kr", 2 * C_ROPE)):
        off[name] = o
        o += w
    return off, o


def _ffn_kernel(x_ref, g_ref, w1_ref, w3_ref, w2_ref, o_ref, xn_ref):
    f = pl.program_id(1)

    @pl.when(f == 0)
    def _():
        xn_ref[...] = _rms(x_ref[...], g_ref[...]).astype(BF16)
        o_ref[...] = jnp.zeros_like(o_ref)

    xn = xn_ref[...]
    a = jnp.dot(xn, w1_ref[...], preferred_element_type=F32)
    b = jnp.dot(xn, w3_ref[...], preferred_element_type=F32)
    hid = (a * _sigmoid(a) * b).astype(BF16)
    o_ref[...] += jnp.dot(hid, w2_ref[...], preferred_element_type=F32)

    @pl.when(f == pl.num_programs(1) - 1)
    def _():
        o_ref[...] = x_ref[...] + HALF_STEP * o_ref[...]


def _ffn(h, g, w13, w2, li):
    t, d = h.shape
    d_ff = w2.shape[1]
    tm = _tile(t, 1024)
    tf = _tile(d_ff, 512)
    nf = d_ff // tf
    return pl.pallas_call(
        _ffn_kernel,
        out_shape=jax.ShapeDtypeStruct((t, d), F32),
        grid=(t // tm, nf),
        in_specs=[pl.BlockSpec((tm, d), lambda i, f: (i, 0)),
                  pl.BlockSpec((None, 1, d), lambda i, f: (li, 0, 0)),
                  pl.BlockSpec((None, d, tf), lambda i, f: (li, 0, f)),
                  pl.BlockSpec((None, d, tf), lambda i, f: (li, 0, nf + f)),
                  pl.BlockSpec((None, tf, d), lambda i, f: (li, f, 0))],
        out_specs=pl.BlockSpec((tm, d), lambda i, f: (i, 0)),
        scratch_shapes=[pltpu.VMEM((tm, d), BF16)],
        compiler_params=_cparams("parallel", "arbitrary"),
        name="ffn",
    )(h, g, w13, w13, w2)


def _gate_act(z, idx):
    return jnp.where(idx < 2 * A_HEADS, z,
                     jnp.where(idx < 4 * A_HEADS, -_softplus(-z), _softplus(z)))


def _proj_kernel(x_ref, g_ref, w_ref, wg_ref, bc_ref, br_ref, o_ref, gc_ref, gr_ref, xn_ref):
    @pl.when(pl.program_id(1) == 0)
    def _():
        xn = _rms(x_ref[...], g_ref[...]).astype(BF16)
        xn_ref[...] = xn
        zc = lax.dot_general(xn, wg_ref[...], NT_DIMS, preferred_element_type=F32) + bc_ref[...]
        gc_ref[...] = _gate_act(zc, lax.broadcasted_iota(jnp.int32, zc.shape, 1))
        zr = lax.dot_general(wg_ref[0:N_GATE_ROWS, :], xn, NT_DIMS, preferred_element_type=F32) + br_ref[...]
        gr_ref[...] = _gate_act(zr, lax.broadcasted_iota(jnp.int32, zr.shape, 0))

    o_ref[...] = lax.dot_general(xn_ref[...], w_ref[...], NT_DIMS, preferred_element_type=F32).astype(o_ref.dtype)


def _proj(h, g, w_main, w_gate, bias_c, bias_r, li):
    t, d = h.shape
    n = w_main.shape[1]
    tm = _tile(t, 1024)
    tn = _tile(n, 1024)
    return pl.pallas_call(
        _proj_kernel,
        out_shape=(jax.ShapeDtypeStruct((t, n), BF16),
                   jax.ShapeDtypeStruct((t, GATE_LANES), F32),
                   jax.ShapeDtypeStruct((N_GATE_ROWS, t), F32)),
        grid=(t // tm, n // tn),
        in_specs=[pl.BlockSpec((tm, d), lambda i, j: (i, 0)),
                  pl.BlockSpec((None, 1, d), lambda i, j: (li, 0, 0)),
                  pl.BlockSpec((None, tn, d), lambda i, j: (li, j, 0)),
                  pl.BlockSpec((None, GATE_LANES, d), lambda i, j: (li, 0, 0)),
                  pl.BlockSpec((None, 1, GATE_LANES), lambda i, j: (li, 0, 0)),
                  pl.BlockSpec((None, N_GATE_ROWS, 1), lambda i, j: (li, 0, 0))],
        out_specs=(pl.BlockSpec((tm, tn), lambda i, j: (i, j)),
                   pl.BlockSpec((tm, GATE_LANES), lambda i, j: (i, 0)),
                   pl.BlockSpec((N_GATE_ROWS, tm), lambda i, j: (0, i))),
        scratch_shapes=[pltpu.VMEM((tm, d), BF16)],
        compiler_params=_cparams("parallel", "arbitrary"),
        name="in_proj",
    )(h, g, w_main, w_gate, bias_c, bias_r)


def _chunk_mask(reverse):
    ti = lax.broadcasted_iota(jnp.int32, (CHUNK, CHUNK), 0)
    si = lax.broadcasted_iota(jnp.int32, (CHUNK, CHUNK), 1)
    return (si >= ti) if reverse else (si <= ti)


def _mlstm_kernel(*refs, reverse, final):
    if final:
        (q_ref, k_ref, v_ref, gc_ref, gr_ref, o_ref, hf_ref, ng_ref,
         out_ref, c_ref, n_ref, m_ref) = refs
    else:
        q_ref, k_ref, v_ref, gc_ref, gr_ref, out_ref, c_ref, n_ref, m_ref = refs

    @pl.when(pl.program_id(1) == 0)
    def _():
        c_ref[...] = jnp.zeros_like(c_ref)
        n_ref[...] = jnp.zeros_like(n_ref)
        m_ref[...] = jnp.zeros_like(m_ref)

    allowed = _chunk_mask(reverse)
    amat = allowed.astype(F32)
    scale = A_QK ** -0.5
    d = 1 if reverse else 0
    n_sub = q_ref.shape[1] // CHUNK
    m_st = [m_ref[hd][:, 0:1] for hd in range(A_HEADS)]
    c_st = [c_ref[hd] for hd in range(A_HEADS)]
    n_st = [n_ref[hd] for hd in range(A_HEADS)]
    for cc in (reversed(range(n_sub)) if reverse else range(n_sub)):
        rows = slice(cc * CHUNK, (cc + 1) * CHUNK)
        gc = gc_ref[0, rows, :]
        gr = gr_ref[:, rows]
        cum_c = jnp.dot(amat, gc, precision=HI, preferred_element_type=F32)
        cum_r = lax.dot_general(gr, amat, NT_DIMS, precision=HI, preferred_element_type=F32)
        for hd in range(A_HEADS):
            ii = d * A_HEADS + hd
            fi = 2 * A_HEADS + d * A_HEADS + hd
            li_c = gc[:, ii:ii + 1]
            li_r = gr[ii:ii + 1, :]
            b_c = cum_c[:, fi:fi + 1]
            b_r = cum_r[fi:fi + 1, :]
            g_tot = jnp.sum(gc[:, fi:fi + 1], axis=0, keepdims=True)
            m0, c0, n0 = m_st[hd], c_st[hd], n_st[hd]
            qh = q_ref[0, rows, hd * A_QK:(hd + 1) * A_QK]
            kh = k_ref[0, rows, hd * A_QK:(hd + 1) * A_QK]
            vh = v_ref[0, rows, hd * A_V:(hd + 1) * A_V]
            kf = kh.astype(F32)
            w_state = g_tot - b_c + li_c
            m_loc = jnp.max(w_state, axis=0, keepdims=True)
            es = jnp.exp(w_state - m_loc) * scale
            ke = kf * es
            c_loc = lax.dot_general(ke.astype(BF16), vh, TN_DIMS, preferred_element_type=F32)
            n_loc = jnp.sum(ke, axis=0, keepdims=True)
            r_sel = jnp.where(allowed, li_r - b_r, -jnp.inf)
            mm = jnp.maximum(m0, jnp.max(r_sel, axis=1, keepdims=True))
            m_t = b_c + mm
            qk = lax.dot_general(qh, kh, NT_DIMS, preferred_element_type=F32)
            s_qk = qk * (jnp.exp(r_sel - mm) * scale)
            e_inter = jnp.exp(m0 - mm)
            num = (jnp.dot(s_qk.astype(BF16), vh, preferred_element_type=F32)
                   + e_inter * jnp.dot(qh, c0.astype(BF16), preferred_element_type=F32))
            den = (jnp.sum(s_qk, axis=1, keepdims=True)
                   + e_inter * jnp.sum(qh.astype(F32) * n0, axis=1, keepdims=True))
            hout = num / jnp.maximum(jnp.abs(den), jnp.exp(-m_t))
            m_new = jnp.maximum(g_tot + m0, m_loc)
            a_prev = jnp.exp(g_tot + m0 - m_new)
            a_loc = jnp.exp(m_loc - m_new)
            c_st[hd] = a_prev * c0 + a_loc * c_loc
            n_st[hd] = a_prev * n0 + a_loc * n_loc
            m_st[hd] = m_new
            sl = slice(hd * A_V, (hd + 1) * A_V)
            if final:
                htot = hout + hf_ref[0, rows, sl]
                hn = htot * lax.rsqrt(jnp.mean(htot * htot, axis=-1, keepdims=True) + NORM_EPS)
                out_ref[0, rows, sl] = (_sigmoid(o_ref[0, rows, sl].astype(F32)) * hn
                                        * ng_ref[:, sl]).astype(out_ref.dtype)
            else:
                out_ref[0, rows, sl] = hout
    for hd in range(A_HEADS):
        c_ref[hd] = c_st[hd]
        n_ref[hd] = n_st[hd]
        m_ref[hd] = jnp.broadcast_to(m_st[hd], (1, GATE_LANES))


def _mlstm_dir(main3, gc3, gr, off, reverse, h_fwd=None, norm_g=None, li=0):
    bsz, seq, _ = main3.shape
    rows = SCAN_CHUNKS_PER_STEP * CHUNK
    nc = seq // rows
    final = h_fwd is not None

    def cidx(c):
        return (nc - 1 - c) if reverse else c

    def col(name, width):
        blk = off[name] // width
        return pl.BlockSpec((1, rows, width), lambda b, c: (b, cidx(c), blk))

    in_specs = [col("a_q", A_QK_W), col("a_k", A_QK_W), col("a_v", A_V_W),
                pl.BlockSpec((1, rows, GATE_LANES), lambda b, c: (b, cidx(c), 0)),
                pl.BlockSpec((N_GATE_ROWS, rows), lambda b, c: (0, b * nc + cidx(c)))]
    args = [main3, main3, main3, gc3, gr]
    if final:
        in_specs += [col("a_o", A_V_W),
                     pl.BlockSpec((1, rows, A_V_W), lambda b, c: (b, cidx(c), 0)),
                     pl.BlockSpec((None, 1, A_V_W), lambda b, c: (li, 0, 0))]
        args += [main3, h_fwd, norm_g]
    return pl.pallas_call(
        functools.partial(_mlstm_kernel, reverse=reverse, final=final),
        out_shape=jax.ShapeDtypeStruct((bsz, seq, A_V_W), BF16 if final else F32),
        grid=(bsz, nc),
        in_specs=in_specs,
        out_specs=pl.BlockSpec((1, rows, A_V_W), lambda b, c: (b, cidx(c), 0)),
        scratch_shapes=[pltpu.VMEM((A_HEADS, A_QK, A_V), F32),
                        pltpu.VMEM((A_HEADS, 1, A_QK), F32),
                        pltpu.VMEM((A_HEADS, 1, GATE_LANES), F32)],
        compiler_params=_cparams("parallel", "arbitrary"),
        name="mlstm_bwd" if reverse else "mlstm_fwd",
    )(*args)


def _conv_kernel(xc_ref, xp_ref, xn_ref, bc_ref, bp_ref, bn_ref, w_ref, b_ref, ox_ref, obc_ref, sx_ref, sbc_ref):
    i = pl.program_id(1)
    first = i == 0
    last = i == pl.num_programs(1) - 1
    pad = (CONV_K - 1) // 2

    def run(cur_ref, prev_ref, next_ref, scr_ref, out_ref, c0):
        tb, w = cur_ref.shape[1], cur_ref.shape[2]
        scr_ref[0:8, :] = jnp.where(first, 0.0, prev_ref[0].astype(F32))
        scr_ref[8:8 + tb, :] = cur_ref[0].astype(F32)
        scr_ref[8 + tb:16 + tb, :] = jnp.where(last, 0.0, next_ref[0].astype(F32))
        acc = jnp.zeros((tb, w), F32) + b_ref[:, c0:c0 + w]
        for j in range(CONV_K):
            acc = acc + scr_ref[8 - pad + j:8 - pad + j + tb, :] * w_ref[j:j + 1, c0:c0 + w]
        out_ref[0] = (acc * _sigmoid(acc)).astype(out_ref.dtype)

    run(xc_ref, xp_ref, xn_ref, sx_ref, ox_ref, 0)
    run(bc_ref, bp_ref, bn_ref, sbc_ref, obc_ref, B_INNER)


def _conv(main3, conv_w, conv_b, off, li):
    bsz, seq, _ = main3.shape
    tb = _tile(seq, 512)
    nb = seq // tb
    r8 = tb // 8
    last8 = seq // 8 - 1

    def trio(name, width):
        blk = off[name] // width
        return [pl.BlockSpec((1, tb, width), lambda b, i: (b, i, blk)),
                pl.BlockSpec((1, 8, width), lambda b, i: (b, jnp.maximum(i * r8 - 1, 0), blk)),
                pl.BlockSpec((1, 8, width), lambda b, i: (b, jnp.minimum((i + 1) * r8, last8), blk))]

    wtot = B_INNER + B_BC_W
    return pl.pallas_call(
        _conv_kernel,
        out_shape=(jax.ShapeDtypeStruct((bsz, seq, B_INNER), BF16),
                   jax.ShapeDtypeStruct((bsz, seq, B_BC_W), BF16)),
        grid=(bsz, nb),
        in_specs=trio("b_x", B_INNER) + trio("b_bc", B_BC_W) + [
            pl.BlockSpec((None, CONV_K, wtot), lambda b, i: (li, 0, 0)),
            pl.BlockSpec((None, 1, wtot), lambda b, i: (li, 0, 0))],
        out_specs=(pl.BlockSpec((1, tb, B_INNER), lambda b, i: (b, i, 0)),
                   pl.BlockSpec((1, tb, B_BC_W), lambda b, i: (b, i, 0))),
        scratch_shapes=[pltpu.VMEM((tb + 16, B_INNER), F32), pltpu.VMEM((tb + 16, B_BC_W), F32)],
        compiler_params=_cparams("parallel", "parallel"),
        name="dwconv_silu",
    )(main3, main3, main3, main3, main3, main3, conv_w, conv_b)


def _ssd_kernel(*refs, reverse, final):
    if final:
        (xs_ref, bc_ref, gc_ref, gr_ref, alr_ref, alc_ref, z_ref, yf_ref, dsk_ref, ng_ref,
         out_ref, s_ref, y_ref) = refs
    else:
        xs_ref, bc_ref, gc_ref, gr_ref, alr_ref, alc_ref, out_ref, s_ref, y_ref = refs

    @pl.when(pl.program_id(1) == 0)
    def _():
        s_ref[...] = jnp.zeros_like(s_ref)

    d = 1 if reverse else 0
    col0 = DT_COL0 + d * B_HEADS
    allowed = _chunk_mask(reverse)
    amat = allowed.astype(F32)
    lane = lax.broadcasted_iota(jnp.int32, (1, GATE_LANES), 1)
    a_row = jnp.where((lane >= col0) & (lane < col0 + B_HEADS), -jnp.exp(alr_ref[...]), 0.0)
    row = lax.broadcasted_iota(jnp.int32, (N_GATE_ROWS, 1), 0)
    a_col = jnp.where((row >= col0) & (row < col0 + B_HEADS), -jnp.exp(alc_ref[...]), 0.0)
    er = lax.broadcasted_iota(jnp.int32, (GATE_LANES, B_INNER), 0)
    ec = lax.broadcasted_iota(jnp.int32, (GATE_LANES, B_INNER), 1)
    expand = (er == col0 + (ec >> (B_HEADDIM.bit_length() - 1))).astype(F32)
    expand_b = expand.astype(BF16)

    def spread(t):
        return jnp.dot(t.astype(BF16), expand_b, preferred_element_type=F32)

    gw = B_INNER // B_GROUPS
    hpg = B_HEADS // B_GROUPS
    n_sub = xs_ref.shape[1] // CHUNK
    s_st = [s_ref[g] for g in range(B_GROUPS)]
    for cc in (reversed(range(n_sub)) if reverse else range(n_sub)):
        rows = slice(cc * CHUNK, (cc + 1) * CHUNK)
        gc = gc_ref[0, rows, :]
        gr = gr_ref[:, rows]
        dta_c = gc * a_row
        dta_r = gr * a_col
        acs_c = jnp.dot(amat, dta_c, precision=HI, preferred_element_type=F32)
        acs_r = lax.dot_general(dta_r, amat, NT_DIMS, precision=HI, preferred_element_type=F32)
        tot = jnp.sum(dta_c, axis=0, keepdims=True)
        dt_x = spread(gc)
        to_end_x = spread(jnp.exp(tot - acs_c))
        from_start_x = spread(jnp.exp(acs_c))
        chunk_decay_x = jnp.dot(jnp.broadcast_to(jnp.exp(tot), (8, GATE_LANES)), expand, precision=HI,
                                preferred_element_type=F32)[0:1]
        xs = xs_ref[0, rows, :].astype(F32)
        xdt = xs * dt_x
        xdt_b = xdt.astype(BF16)
        xend_b = (xdt * to_end_x).astype(BF16)
        for g in range(B_GROUPS):
            bm = bc_ref[0, rows, g * B_STATE:(g + 1) * B_STATE]
            cm = bc_ref[0, rows, (B_GROUPS + g) * B_STATE:(B_GROUPS + g + 1) * B_STATE]
            gs = slice(g * gw, (g + 1) * gw)
            s0 = s_st[g]
            y_ref[rows, gs] = jnp.dot(cm, s0.astype(BF16), preferred_element_type=F32) * from_start_x[:, gs]
            states = lax.dot_general(bm, xend_b[:, gs], TN_DIMS, preferred_element_type=F32)
            s_st[g] = chunk_decay_x[:, gs] * s0 + states
            cb = lax.dot_general(cm, bm, NT_DIMS, preferred_element_type=F32)
            for hh in range(hpg):
                ci = col0 + g * hpg + hh
                seg = acs_c[:, ci:ci + 1] - acs_r[ci:ci + 1, :]
                dec = jnp.exp(jnp.where(allowed, seg, -jnp.inf))
                hs = slice((g * hpg + hh) * B_HEADDIM, (g * hpg + hh + 1) * B_HEADDIM)
                y_ref[rows, hs] += jnp.dot((cb * dec).astype(BF16), xdt_b[:, hs], preferred_element_type=F32)
    for g in range(B_GROUPS):
        s_ref[g] = s_st[g]
    if final:
        y = y_ref[...] + yf_ref[0] + dsk_ref[...] * xs_ref[0].astype(F32)
        zf = z_ref[0].astype(F32)
        out_ref[0] = _rms(y * (zf * _sigmoid(zf)), ng_ref[...]).astype(out_ref.dtype)
    else:
        out_ref[0] = y_ref[...]


def _ssd_dir(xs3, bc3, gc3, gr, alog_r, alog_c, reverse, li, main3=None, off=None, y_fwd=None, d_skip=None,
             norm_g=None):
    bsz, seq, _ = xs3.shape
    rows = SCAN_CHUNKS_PER_STEP * CHUNK
    nc = seq // rows
    final = y_fwd is not None

    def cidx(c):
        return (nc - 1 - c) if reverse else c

    def seq_spec(width, blk=0):
        return pl.BlockSpec((1, rows, width), lambda b, c: (b, cidx(c), blk))

    def const_spec(shape):
        return pl.BlockSpec((None,) + shape, lambda b, c: (li, 0, 0))

    in_specs = [seq_spec(B_INNER), seq_spec(B_BC_W), seq_spec(GATE_LANES),
                pl.BlockSpec((N_GATE_ROWS, rows), lambda b, c: (0, b * nc + cidx(c))),
                const_spec((1, GATE_LANES)), const_spec((N_GATE_ROWS, 1))]
    args = [xs3, bc3, gc3, gr, alog_r, alog_c]
    if final:
        in_specs += [seq_spec(B_INNER, off["b_z"] // B_INNER), seq_spec(B_INNER),
                     const_spec((1, B_INNER)), const_spec((1, B_INNER))]
        args += [main3, y_fwd, d_skip, norm_g]
    return pl.pallas_call(
        functools.partial(_ssd_kernel, reverse=reverse, final=final),
        out_shape=jax.ShapeDtypeStruct((bsz, seq, B_INNER), BF16 if final else F32),
        grid=(bsz, nc),
        in_specs=in_specs,
        out_specs=seq_spec(B_INNER),
        scratch_shapes=[pltpu.VMEM((B_GROUPS, B_STATE, B_INNER // B_GROUPS), F32),
                        pltpu.VMEM((rows, B_INNER), F32)],
        compiler_params=_cparams("parallel", "arbitrary"),
        name="ssd_bwd" if reverse else "ssd_fwd",
    )(*args)


def _rope_kernel(pos_ref, tab_ref):
    half = C_ROPE // 2
    lane = lax.broadcasted_iota(jnp.int32, (1, 4 * half), 1)
    fr = (lane & (half - 1)).astype(F32)
    inv_freq = jnp.exp(fr * (-math.log(ROPE_THETA) / half))
    ang = pos_ref[...].astype(F32) * inv_freq
    sn = jnp.sin(ang)
    tab_ref[...] = jnp.where(lane < 2 * half, jnp.cos(ang), jnp.where(lane < 3 * half, -sn, sn))


def _rope_table(pos_col):
    t = pos_col.shape[0]
    tm = _tile(t, 1024)
    return pl.pallas_call(
        _rope_kernel,
        out_shape=jax.ShapeDtypeStruct((t, 2 * C_ROPE), F32),
        grid=(t // tm,),
        in_specs=[pl.BlockSpec((tm, 1), lambda i: (i, 0))],
        out_specs=pl.BlockSpec((tm, 2 * C_ROPE), lambda i: (i, 0)),
        compiler_params=_cparams("parallel"),
        name="rope_table",
    )(pos_col)


def _mla_prep_kernel(cq_ref, ckv_ref, ckr_ref, tab_ref, qg_ref, kvg_ref, wq_ref, wkv_ref, q_ref, k_ref, v_ref):
    tab = tab_ref[0]
    lane = lax.broadcasted_iota(jnp.int32, tab.shape, 1)

    def rope(t, zero_upper):
        p = t * tab
        r = p + pltpu.roll(p, C_ROPE, axis=1)
        return jnp.where(lane < C_ROPE, r, 0.0) if zero_upper else r

    cqn = _rms(cq_ref[0].astype(F32), qg_ref[...]).astype(BF16)
    ckvn = _rms(ckv_ref[0].astype(F32), kvg_ref[...]).astype(BF16)
    q_all = jnp.dot(cqn, wq_ref[...], preferred_element_type=F32)
    kv_all = jnp.dot(ckvn, wkv_ref[...], preferred_element_type=F32)
    k_rot = rope(ckr_ref[0].astype(F32), False).astype(BF16)
    qscale = (C_NOPE + C_ROPE) ** -0.5 * math.log2(math.e)
    ones_col = (lane == 0).astype(BF16)
    for hd in range(C_HEADS):
        q0 = hd * C_QK_PAD
        q_ref[0, hd, :, 0:C_NOPE] = (q_all[:, q0:q0 + C_NOPE] * qscale).astype(BF16)
        q_ref[0, hd, :, C_NOPE:C_QK_PAD] = (rope(q_all[:, q0 + C_NOPE:q0 + C_QK_PAD], True) * qscale).astype(BF16)
        k_ref[0, hd, :, 0:C_NOPE] = kv_all[:, hd * C_NOPE:(hd + 1) * C_NOPE].astype(BF16)
        k_ref[0, hd, :, C_NOPE:C_QK_PAD] = k_rot
        v0 = C_HEADS * C_NOPE + hd * C_V
        v_ref[0, hd, :, 0:C_V] = kv_all[:, v0:v0 + C_V].astype(BF16)
        v_ref[0, hd, :, C_V:C_V_PAD] = ones_col


def _mla_prep(main3, tab3, q_g, kv_g, wq, wkv, off, li):
    bsz, seq, _ = main3.shape
    tm = _tile(seq, 512)

    def col(name, width):
        blk = off[name] // width
        return pl.BlockSpec((1, tm, width), lambda b, i: (b, i, blk))

    def const_spec(shape):
        return pl.BlockSpec((None,) + tuple(shape), lambda b, i: (li, 0, 0))

    def head_spec(width):
        return pl.BlockSpec((1, C_HEADS, tm, width), lambda b, i: (b, 0, i, 0))

    return pl.pallas_call(
        _mla_prep_kernel,
        out_shape=(jax.ShapeDtypeStruct((bsz, C_HEADS, seq, C_QK_PAD), BF16),
                   jax.ShapeDtypeStruct((bsz, C_HEADS, seq, C_QK_PAD), BF16),
                   jax.ShapeDtypeStruct((bsz, C_HEADS, seq, C_V_PAD), BF16)),
        grid=(bsz, seq // tm),
        in_specs=[col("c_q", C_Q_RANK), col("c_kv", C_KV_RANK), col("c_kr", 2 * C_ROPE),
                  pl.BlockSpec((1, tm, 2 * C_ROPE), lambda b, i: (b, i, 0)),
                  const_spec((1, C_Q_RANK)), const_spec((1, C_KV_RANK)),
                  const_spec(wq.shape[1:]), const_spec(wkv.shape[1:])],
        out_specs=(head_spec(C_QK_PAD), head_spec(C_QK_PAD), head_spec(C_V_PAD)),
        compiler_params=_cparams("parallel", "parallel"),
        name="mla_prep",
    )(main3, main3, main3, tab3, q_g, kv_g, wq, wkv)


def _attn_kernel(q_ref, k_ref, v_ref, o_ref, *, tk):
    q = q_ref[0, 0]
    nk = k_ref.shape[2] // tk
    m = None
    acc = None
    for j in range(nk):
        kb = k_ref[0, 0, j * tk:(j + 1) * tk, :]
        vb = v_ref[0, 0, j * tk:(j + 1) * tk, :]
        s = lax.dot_general(q, kb, NT_DIMS, preferred_element_type=F32)
        m_blk = jnp.max(s, axis=1, keepdims=True)
        m_new = m_blk if m is None else jnp.maximum(m, m_blk)
        pv = jnp.dot(jnp.exp2(s - m_new).astype(BF16), vb, preferred_element_type=F32)
        acc = pv if acc is None else jnp.exp2(m - m_new) * acc + pv
        m = m_new
    o_ref[0] = (acc[:, 0:C_V] / acc[:, C_V:C_V + 1]).astype(o_ref.dtype)


def _attention(q4, k4, v4):
    bsz, nh, seq, _ = q4.shape
    tq = _tile(seq, 1024)
    tk = _tile(seq, 256)
    return pl.pallas_call(
        functools.partial(_attn_kernel, tk=tk),
        out_shape=jax.ShapeDtypeStruct((bsz, seq, nh * C_V), BF16),
        grid=(bsz, nh, seq // tq),
        in_specs=[pl.BlockSpec((1, 1, tq, C_QK_PAD), lambda b, h, i: (b, h, i, 0)),
                  pl.BlockSpec((1, 1, seq, C_QK_PAD), lambda b, h, i: (b, h, 0, 0)),
                  pl.BlockSpec((1, 1, seq, C_V_PAD), lambda b, h, i: (b, h, 0, 0))],
        out_specs=pl.BlockSpec((1, tq, C_V), lambda b, h, i: (b, i, h)),
        compiler_params=_cparams("parallel", "parallel", "parallel"),
        name="mla_attention",
    )(q4, k4, v4)


def _merge_kernel(ya_ref, yb_ref, yc_ref, wb_ref, g0_ref, g1_ref, g2_ref, o_ref):
    acc = None
    for y_ref, g_ref, k in ((ya_ref, g0_ref, 0), (yb_ref, g1_ref, 1), (yc_ref, g2_ref, 2)):
        t = _sigmoid(g_ref[...].astype(F32)) * jnp.dot(y_ref[...], wb_ref[k], preferred_element_type=F32)
        acc = t if acc is None else acc + t
    o_ref[...] = acc.astype(o_ref.dtype)


def _merge(ya, yb, yc, wb, main, off, d_model, li):
    t = ya.shape[0]
    tm = _tile(t, 1024)
    tn = _tile(d_model, 1024)
    gblk = off["gate"] // tn
    per = d_model // tn

    def gate_spec(k):
        return pl.BlockSpec((tm, tn), lambda i, j: (i, gblk + k * per + j))

    y_spec = pl.BlockSpec((tm, BRANCH_W), lambda i, j: (i, 0))
    return pl.pallas_call(
        _merge_kernel,
        out_shape=jax.ShapeDtypeStruct((t, d_model), BF16),
        grid=(t // tm, per),
        in_specs=[y_spec, y_spec, y_spec,
                  pl.BlockSpec((None, N_BRANCH, BRANCH_W, tn), lambda i, j: (li, 0, 0, j)),
                  gate_spec(0), gate_spec(1), gate_spec(2)],
        out_specs=pl.BlockSpec((tm, tn), lambda i, j: (i, j)),
        compiler_params=_cparams("parallel", "parallel"),
        name="branch_merge",
    )(ya, yb, yc, wb, main, main, main)


def _mm_res_kernel(a_ref, w_ref, r_ref, o_ref):
    o_ref[...] = r_ref[...] + jnp.dot(a_ref[...], w_ref[...], preferred_element_type=F32)


def _mm_res(a, w, res, li):
    t, k = a.shape
    n = w.shape[2]
    tm = _tile(t, 1024)
    tn = _tile(n, 1024)
    return pl.pallas_call(
        _mm_res_kernel,
        out_shape=jax.ShapeDtypeStruct((t, n), F32),
        grid=(t // tm, n // tn),
        in_specs=[pl.BlockSpec((tm, k), lambda i, j: (i, 0)),
                  pl.BlockSpec((None, k, tn), lambda i, j: (li, 0, j)),
                  pl.BlockSpec((tm, tn), lambda i, j: (i, j))],
        out_specs=pl.BlockSpec((tm, tn), lambda i, j: (i, j)),
        compiler_params=_cparams("parallel", "parallel"),
        name="out_proj",
    )(a, w, res)


def _ple_kernel(x_ref, g_ref, wg_ref, p_ref, wp_ref, *rest, last):
    x = x_ref[...]
    xn = _rms(x, g_ref[...]).astype(BF16)
    gate = _sigmoid(jnp.dot(xn, wg_ref[...], preferred_element_type=F32))
    emb = jnp.dot(p_ref[...].astype(BF16), wp_ref[...], preferred_element_type=F32)
    h_new = x + gate * emb
    if last:
        fg_ref, o_ref = rest
        o_ref[...] = _rms(h_new, fg_ref[...])
    else:
        rest[0][...] = h_new


def _ple(h, g, wgate, p, wproj, li, final_g=None):
    t, d = h.shape
    pd = p.shape[2]
    tm = _tile(t, 512)
    last = final_g is not None
    in_specs = [pl.BlockSpec((tm, d), lambda i: (i, 0)),
                pl.BlockSpec((None, 1, d), lambda i: (li, 0, 0)),
                pl.BlockSpec((None, d, d), lambda i: (li, 0, 0)),
                pl.BlockSpec((None, tm, pd), lambda i: (li, i, 0)),
                pl.BlockSpec((None, pd, d), lambda i: (li, 0, 0))]
    args = [h, g, wgate, p, wproj]
    if last:
        in_specs.append(pl.BlockSpec((1, d), lambda i: (0, 0)))
        args.append(final_g)
    return pl.pallas_call(
        functools.partial(_ple_kernel, last=last),
        out_shape=jax.ShapeDtypeStruct((t, d), F32),
        grid=(t // tm,),
        in_specs=in_specs,
        out_specs=pl.BlockSpec((tm, d), lambda i: (i, 0)),
        compiler_params=_cparams("parallel"),
        name="ple",
    )(*args)


def _prep_w_in(w_in, d_model):
    sizes = (A_QK_W, A_QK_W, A_V_W, A_V_W, 2 * A_HEADS, 2 * A_HEADS, B_INNER, B_INNER + B_BC_W,
             2 * B_HEADS, C_Q_RANK, C_KV_RANK, C_ROPE, N_BRANCH * d_model)
    w_t = jnp.swapaxes(w_in, 1, 2)
    cuts, o = [], 0
    for s in sizes:
        cuts.append((o, o + s))
        o += s
    (a_q, a_k, a_v, a_o, a_ig, a_fg, b_z, b_xbc, b_dt, c_q, c_kv, c_kr, gate) = [w_t[:, lo:hi] for lo, hi in cuts]
    half = C_ROPE // 2
    pieces = {"a_v": a_v, "a_o": a_o, "b_z": b_z, "b_x": b_xbc[:, :B_INNER], "gate": gate, "a_q": a_q,
              "a_k": a_k, "b_bc": b_xbc[:, B_INNER:], "c_q": c_q, "c_kv": c_kv,
              "c_kr": jnp.concatenate([c_kr, c_kr[:, half:], c_kr[:, :half]], axis=1)}
    off, n1 = _layout(d_model)
    order = sorted(off, key=off.get)
    n_pad = -n1 % 1024
    rows = [pieces[k] for k in order]
    if n_pad:
        rows.append(jnp.zeros((w_t.shape[0], n_pad, w_t.shape[2]), w_t.dtype))
    w_main = jnp.concatenate(rows, axis=1).astype(BF16)
    w_gate = jnp.concatenate([a_ig, a_fg, b_dt,
                              jnp.zeros((w_t.shape[0], GATE_LANES - N_GATE_ROWS, w_t.shape[2]), w_t.dtype)],
                             axis=1).astype(BF16)
    return w_main, w_gate, off


def _prep_mla(w_uq, w_ukv):
    nl, rq, _ = w_uq.shape
    half = C_ROPE // 2
    wq = w_uq.reshape(nl, rq, C_HEADS, C_NOPE + C_ROPE)
    t1 = wq[..., C_NOPE:C_NOPE + half]
    t2 = wq[..., C_NOPE + half:]
    wq = jnp.concatenate([wq[..., :C_NOPE], t1, t2, t2, t1], axis=-1).reshape(nl, rq, C_HEADS * C_QK_PAD)
    wkv = w_ukv.reshape(nl, w_ukv.shape[1], C_HEADS, C_NOPE + C_V)
    wkv = jnp.concatenate([wkv[..., :C_NOPE].reshape(nl, -1, C_HEADS * C_NOPE),
                           wkv[..., C_NOPE:].reshape(nl, -1, C_HEADS * C_V)], axis=-1)
    return wq.astype(BF16), wkv.astype(BF16)


def kernel(x, p, positions, ffn1_norm, ffn1_w13, ffn1_w2, mix_norm, w_in, mlstm_b_igate, mlstm_b_fgate, mlstm_norm, conv_w, conv_b, ssm_a_log, ssm_dt_bias, ssm_d, ssm_norm, mla_q_norm, mla_kv_norm, mla_w_uq, mla_w_ukv, w_branch, w_out, ffn2_norm, ffn2_w13, ffn2_w2, ple_norm, w_ple_gate, w_ple_proj, final_norm):
    bsz, seq, d = x.shape
    depth = w_in.shape[0]
    t = bsz * seq
    nl = depth

    w_main, w_gate, off = _prep_w_in(w_in, d)
    wq, wkv = _prep_mla(mla_w_uq, mla_w_ukv)
    ffn1_w13b, ffn1_w2b = ffn1_w13.astype(BF16), ffn1_w2.astype(BF16)
    ffn2_w13b, ffn2_w2b = ffn2_w13.astype(BF16), ffn2_w2.astype(BF16)
    w_branch_b, w_out_b = w_branch.astype(BF16), w_out.astype(BF16)
    w_pg_b, w_pp_b = w_ple_gate.astype(BF16), w_ple_proj.astype(BF16)
    gate_bias = jnp.concatenate([mlstm_b_igate.reshape(nl, -1), mlstm_b_fgate.reshape(nl, -1),
                                 ssm_dt_bias.reshape(nl, -1)], axis=-1).astype(F32)
    bias_c = jnp.pad(gate_bias, ((0, 0), (0, GATE_LANES - N_GATE_ROWS)))[:, None, :]
    bias_r = gate_bias[:, :, None]
    alog = ssm_a_log.reshape(nl, -1).astype(F32)
    alog_r = jnp.pad(alog, ((0, 0), (DT_COL0, GATE_LANES - N_GATE_ROWS)))[:, None, :]
    alog_c = jnp.pad(alog, ((0, 0), (DT_COL0, 0)))[:, :, None]
    d_skip = jnp.repeat(ssm_d.astype(F32), B_HEADDIM, axis=-1)[:, None, :]
    row = lambda v: v[:, None, :]
    p3 = p.reshape(nl, t, -1)

    tab3 = _rope_table(positions.reshape(t, 1)).reshape(bsz, seq, 2 * C_ROPE)

    h = x.reshape(t, d)
    for i in range(depth):
        h = _ffn(h, row(ffn1_norm), ffn1_w13b, ffn1_w2b, i)
        main, gc, gr = _proj(h, row(mix_norm), w_main, w_gate, bias_c, bias_r, i)
        main3 = main.reshape(bsz, seq, -1)
        gc3 = gc.reshape(bsz, seq, GATE_LANES)
        hf = _mlstm_dir(main3, gc3, gr, off, False)
        ya = _mlstm_dir(main3, gc3, gr, off, True, h_fwd=hf, norm_g=row(mlstm_norm), li=i)
        xs3, bc3 = _conv(main3, conv_w, row(conv_b), off, i)
        yf = _ssd_dir(xs3, bc3, gc3, gr, alog_r, alog_c, False, i)
        yb = _ssd_dir(xs3, bc3, gc3, gr, alog_r, alog_c, True, i, main3=main3, off=off, y_fwd=yf,
                      d_skip=d_skip, norm_g=row(ssm_norm))
        q4, k4, v4 = _mla_prep(main3, tab3, row(mla_q_norm), row(mla_kv_norm), wq, wkv, off, i)
        yc = _attention(q4, k4, v4)
        merged = _merge(ya.reshape(t, -1), yb.reshape(t, -1), yc.reshape(t, -1), w_branch_b, main, off, d, i)
        h = _mm_res(merged, w_out_b, h, i)
        h = _ffn(h, row(ffn2_norm), ffn2_w13b, ffn2_w2b, i)
        h = _ple(h, row(ple_norm), w_pg_b, p3, w_pp_b, i, final_g=final_norm[None] if i == depth - 1 else None)
    return h.reshape(bsz, seq, d)
```

```python
import functools
import math

import jax
import jax.numpy as jnp
from jax import lax
from jax.experimental import pallas as pl
from jax.experimental.pallas import tpu as pltpu

F32 = jnp.float32
BF16 = jnp.bfloat16

HALF_STEP = 0.5
NORM_EPS = 1e-6
CHUNK = 128
N_BRANCH = 3
A_HEADS = 4
A_QK = 128
A_V = 256
B_HEADS = 16
B_HEADDIM = 64
B_INNER = B_HEADS * B_HEADDIM
B_STATE = 128
B_GROUPS = 2
CONV_K = 5
C_HEADS = 8
C_Q_RANK = 512
C_KV_RANK = 512
C_NOPE = 128
C_ROPE = 64
C_V = 128
C_QK_PAD = 256
C_V_PAD = 256
ROPE_THETA = 10000.0
BRANCH_W = 1024
A_QK_W = A_HEADS * A_QK
A_V_W = A_HEADS * A_V
B_BC_W = 2 * B_GROUPS * B_STATE
N_GATE_ROWS = 2 * A_HEADS + 2 * A_HEADS + 2 * B_HEADS
GATE_LANES = 128
DT_COL0 = 4 * A_HEADS
SCAN_CHUNKS_PER_STEP = 8
VMEM_LIMIT = 56 * 1024 * 1024
HI = lax.Precision.HIGHEST
NT_DIMS = (((1,), (1,)), ((), ()))
TN_DIMS = (((0,), (0,)), ((), ()))


def _cparams(*sem):
    return pltpu.CompilerParams(dimension_semantics=sem, vmem_limit_bytes=VMEM_LIMIT)


def _tile(n, pref):
    t = min(n, pref)
    while n % t:
        t //= 2
    return t


def _softplus(x):
    return jnp.maximum(x, 0.0) + jnp.log1p(jnp.exp(-jnp.abs(x)))


def _sigmoid(x):
    return 1.0 / (1.0 + jnp.exp(-x))


def _rms(x, g):
    return x * lax.rsqrt(jnp.mean(x * x, axis=-1, keepdims=True) + NORM_EPS) * g


def _layout(d_model):
    off = {}
    o = 0
    for name, w in (("a_v", A_V_W), ("a_o", A_V_W), ("b_z", B_INNER), ("b_x", B_INNER),
                    ("gate", N_BRANCH * d_model), ("a_q", A_QK_W), ("a_k", A_QK_W),
                    ("b_bc", B_BC_W), ("c_q", C_Q_RANK), ("c_kv", C_KV_RANK), ("c_kr", 2 * C_ROPE)):
        off[name] = o
        o += w
    return off, o


def _ffn_kernel(x_ref, g_ref, w1_ref, w3_ref, w2_ref, o_ref, xn_ref):
    f = pl.program_id(1)

    @pl.when(f == 0)
    def _():
        xn_ref[...] = _rms(x_ref[...], g_ref[...]).astype(BF16)
        o_ref[...] = jnp.zeros_like(o_ref)

    xn = xn_ref[...]
    a = jnp.dot(xn, w1_ref[...], preferred_element_type=F32)
    b = jnp.dot(xn, w3_ref[...], preferred_element_type=F32)
    hid = (a * _sigmoid(a) * b).astype(BF16)
    o_ref[...] += jnp.dot(hid, w2_ref[...], preferred_element_type=F32)

    @pl.when(f == pl.num_programs(1) - 1)
    def _():
        o_ref[...] = x_ref[...] + HALF_STEP * o_ref[...]


def _ffn(h, g, w13, w2, li):
    t, d = h.shape
    d_ff = w2.shape[1]
    tm = _tile(t, 1024)
    tf = _tile(d_ff, 512)
    nf = d_ff // tf
    return pl.pallas_call(
        _ffn_kernel,
        out_shape=jax.ShapeDtypeStruct((t, d), F32),
        grid=(t // tm, nf),
        in_specs=[pl.BlockSpec((tm, d), lambda i, f: (i, 0)),
                  pl.BlockSpec((None, 1, d), lambda i, f: (li, 0, 0)),
                  pl.BlockSpec((None, d, tf), lambda i, f: (li, 0, f)),
                  pl.BlockSpec((None, d, tf), lambda i, f: (li, 0, nf + f)),
                  pl.BlockSpec((None, tf, d), lambda i, f: (li, f, 0))],
        out_specs=pl.BlockSpec((tm, d), lambda i, f: (i, 0)),
        scratch_shapes=[pltpu.VMEM((tm, d), BF16)],
        compiler_params=_cparams("parallel", "arbitrary"),
        name="ffn",
    )(h, g, w13, w13, w2)


def _gate_act(z, idx):
    return jnp.where(idx < 2 * A_HEADS, z,
                     jnp.where(idx < 4 * A_HEADS, -_softplus(-z), _softplus(z)))


def _proj_kernel(x_ref, g_ref, w_ref, wg_ref, bc_ref, br_ref, o_ref, gc_ref, gr_ref, xn_ref):
    @pl.when(pl.program_id(1) == 0)
    def _():
        xn = _rms(x_ref[...], g_ref[...]).astype(BF16)
        xn_ref[...] = xn
        zc = lax.dot_general(xn, wg_ref[...], NT_DIMS, preferred_element_type=F32) + bc_ref[...]
        gc_ref[...] = _gate_act(zc, lax.broadcasted_iota(jnp.int32, zc.shape, 1))
        zr = lax.dot_general(wg_ref[0:N_GATE_ROWS, :], xn, NT_DIMS, preferred_element_type=F32) + br_ref[...]
        gr_ref[...] = _gate_act(zr, lax.broadcasted_iota(jnp.int32, zr.shape, 0))

    o_ref[...] = lax.dot_general(xn_ref[...], w_ref[...], NT_DIMS, preferred_element_type=F32).astype(o_ref.dtype)


def _proj(h, g, w_main, w_gate, bias_c, bias_r, li):
    t, d = h.shape
    n = w_main.shape[1]
    tm = _tile(t, 1024)
    tn = _tile(n, 1024)
    return pl.pallas_call(
        _proj_kernel,
        out_shape=(jax.ShapeDtypeStruct((t, n), BF16),
                   jax.ShapeDtypeStruct((t, GATE_LANES), F32),
                   jax.ShapeDtypeStruct((N_GATE_ROWS, t), F32)),
        grid=(t // tm, n // tn),
        in_specs=[pl.BlockSpec((tm, d), lambda i, j: (i, 0)),
                  pl.BlockSpec((None, 1, d), lambda i, j: (li, 0, 0)),
                  pl.BlockSpec((None, tn, d), lambda i, j: (li, j, 0)),
                  pl.BlockSpec((None, GATE_LANES, d), lambda i, j: (li, 0, 0)),
                  pl.BlockSpec((None, 1, GATE_LANES), lambda i, j: (li, 0, 0)),
                  pl.BlockSpec((None, N_GATE_ROWS, 1), lambda i, j: (li, 0, 0))],
        out_specs=(pl.BlockSpec((tm, tn), lambda i, j: (i, j)),
                   pl.BlockSpec((tm, GATE_LANES), lambda i, j: (i, 0)),
                   pl.BlockSpec((N_GATE_ROWS, tm), lambda i, j: (0, i))),
        scratch_shapes=[pltpu.VMEM((tm, d), BF16)],
        compiler_params=_cparams("parallel", "arbitrary"),
        name="in_proj",
    )(h, g, w_main, w_gate, bias_c, bias_r)


def _chunk_mask(reverse):
    ti = lax.broadcasted_iota(jnp.int32, (CHUNK, CHUNK), 0)
    si = lax.broadcasted_iota(jnp.int32, (CHUNK, CHUNK), 1)
    return (si >= ti) if reverse else (si <= ti)


def _mlstm_kernel(*refs, reverse, final):
    if final:
        (q_ref, k_ref, v_ref, gc_ref, gr_ref, o_ref, hf_ref, ng_ref,
         out_ref, c_ref, n_ref, m_ref) = refs
    else:
        q_ref, k_ref, v_ref, gc_ref, gr_ref, out_ref, c_ref, n_ref, m_ref = refs

    @pl.when(pl.program_id(1) == 0)
    def _():
        c_ref[...] = jnp.zeros_like(c_ref)
        n_ref[...] = jnp.zeros_like(n_ref)
        m_ref[...] = jnp.zeros_like(m_ref)

    allowed = _chunk_mask(reverse)
    amat = allowed.astype(F32)
    scale = A_QK ** -0.5
    d = 1 if reverse else 0
    n_sub = q_ref.shape[1] // CHUNK
    m_st = [m_ref[hd][:, 0:1] for hd in range(A_HEADS)]
    c_st = [c_ref[hd] for hd in range(A_HEADS)]
    n_st = [n_ref[hd] for hd in range(A_HEADS)]
    for cc in (reversed(range(n_sub)) if reverse else range(n_sub)):
        rows = slice(cc * CHUNK, (cc + 1) * CHUNK)
        gc = gc_ref[0, rows, :]
        gr = gr_ref[:, rows]
        cum_c = jnp.dot(amat, gc, precision=HI, preferred_element_type=F32)
        cum_r = lax.dot_general(gr, amat, NT_DIMS, precision=HI, preferred_element_type=F32)
        for hd in range(A_HEADS):
            ii = d * A_HEADS + hd
            fi = 2 * A_HEADS + d * A_HEADS + hd
            li_c = gc[:, ii:ii + 1]
            li_r = gr[ii:ii + 1, :]
            b_c = cum_c[:, fi:fi + 1]
            b_r = cum_r[fi:fi + 1, :]
            g_tot = jnp.sum(gc[:, fi:fi + 1], axis=0, keepdims=True)
            m0, c0, n0 = m_st[hd], c_st[hd], n_st[hd]
            qh = q_ref[0, rows, hd * A_QK:(hd + 1) * A_QK]
            kh = k_ref[0, rows, hd * A_QK:(hd + 1) * A_QK]
            vh = v_ref[0, rows, hd * A_V:(hd + 1) * A_V]
            kf = kh.astype(F32)
            w_state = g_tot - b_c + li_c
            m_loc = jnp.max(w_state, axis=0, keepdims=True)
            es = jnp.exp(w_state - m_loc) * scale
            ke = kf * es
            c_loc = lax.dot_general(ke.astype(BF16), vh, TN_DIMS, preferred_element_type=F32)
            n_loc = jnp.sum(ke, axis=0, keepdims=True)
            r_sel = jnp.where(allowed, li_r - b_r, -jnp.inf)
            mm = jnp.maximum(m0, jnp.max(r_sel, axis=1, keepdims=True))
            m_t = b_c + mm
            qk = lax.dot_general(qh, kh, NT_DIMS, preferred_element_type=F32)
            s_qk = qk * (jnp.exp(r_sel - mm) * scale)
            e_inter = jnp.exp(m0 - mm)
            num = (jnp.dot(s_qk.astype(BF16), vh, preferred_element_type=F32)
                   + e_inter * jnp.dot(qh, c0.astype(BF16), preferred_element_type=F32))
            den = (jnp.sum(s_qk, axis=1, keepdims=True)
                   + e_inter * jnp.sum(qh.astype(F32) * n0, axis=1, keepdims=True))
            hout = num / jnp.maximum(jnp.abs(den), jnp.exp(-m_t))
            m_new = jnp.maximum(g_tot + m0, m_loc)
            a_prev = jnp.exp(g_tot + m0 - m_new)
            a_loc = jnp.exp(m_loc - m_new)
            c_st[hd] = a_prev * c0 + a_loc * c_loc
            n_st[hd] = a_prev * n0 + a_loc * n_loc
            m_st[hd] = m_new
            sl = slice(hd * A_V, (hd + 1) * A_V)
            if final:
                htot = hout + hf_ref[0, rows, sl]
                hn = htot * lax.rsqrt(jnp.mean(htot * htot, axis=-1, keepdims=True) + NORM_EPS)
                out_ref[0, rows, sl] = (_sigmoid(o_ref[0, rows, sl].astype(F32)) * hn
                                        * ng_ref[:, sl]).astype(out_ref.dtype)
            else:
                out_ref[0, rows, sl] = hout
    for hd in range(A_HEADS):
        c_ref[hd] = c_st[hd]
        n_ref[hd] = n_st[hd]
        m_ref[hd] = jnp.broadcast_to(m_st[hd], (1, GATE_LANES))


def _mlstm_dir(main3, gc3, gr, off, reverse, h_fwd=None, norm_g=None, li=0):
    bsz, seq, _ = main3.shape
    rows = SCAN_CHUNKS_PER_STEP * CHUNK
    nc = seq // rows
    final = h_fwd is not None

    def cidx(c):
        return (nc - 1 - c) if reverse else c

    def col(name, width):
        blk = off[name] // width
        return pl.BlockSpec((1, rows, width), lambda b, c: (b, cidx(c), blk))

    in_specs = [col("a_q", A_QK_W), col("a_k", A_QK_W), col("a_v", A_V_W),
                pl.BlockSpec((1, rows, GATE_LANES), lambda b, c: (b, cidx(c), 0)),
                pl.BlockSpec((N_GATE_ROWS, rows), lambda b, c: (0, b * nc + cidx(c)))]
    args = [main3, main3, main3, gc3, gr]
    if final:
        in_specs += [col("a_o", A_V_W),
                     pl.BlockSpec((1, rows, A_V_W), lambda b, c: (b, cidx(c), 0)),
                     pl.BlockSpec((None, 1, A_V_W), lambda b, c: (li, 0, 0))]
        args += [main3, h_fwd, norm_g]
    return pl.pallas_call(
        functools.partial(_mlstm_kernel, reverse=reverse, final=final),
        out_shape=jax.ShapeDtypeStruct((bsz, seq, A_V_W), BF16 if final else F32),
        grid=(bsz, nc),
        in_specs=in_specs,
        out_specs=pl.BlockSpec((1, rows, A_V_W), lambda b, c: (b, cidx(c), 0)),
        scratch_shapes=[pltpu.VMEM((A_HEADS, A_QK, A_V), F32),
                        pltpu.VMEM((A_HEADS, 1, A_QK), F32),
                        pltpu.VMEM((A_HEADS, 1, GATE_LANES), F32)],
        compiler_params=_cparams("parallel", "arbitrary"),
        name="mlstm_bwd" if reverse else "mlstm_fwd",
    )(*args)


def _conv_kernel(xc_ref, xp_ref, xn_ref, bc_ref, bp_ref, bn_ref, w_ref, b_ref, ox_ref, obc_ref, sx_ref, sbc_ref):
    i = pl.program_id(1)
    first = i == 0
    last = i == pl.num_programs(1) - 1
    pad = (CONV_K - 1) // 2

    taps = [j for j in range(CONV_K) if j != pad]
    win = CHUNK + 16
    sr = lax.broadcasted_iota(jnp.int32, (len(taps) * CHUNK, win), 0)
    sc = lax.broadcasted_iota(jnp.int32, (len(taps) * CHUNK, win), 1)
    sel = None
    for k, j in enumerate(taps):
        hit = (sr >= k * CHUNK) & (sr < (k + 1) * CHUNK) & (sc == sr - k * CHUNK + 8 + j - pad)
        sel = hit if sel is None else (sel | hit)
    sel = sel.astype(BF16)

    def run(cur_ref, prev_ref, next_ref, scr_ref, out_ref, c0):
        tb, w = cur_ref.shape[1], cur_ref.shape[2]
        scr_ref[0:8, :] = jnp.where(first, 0.0, prev_ref[0].astype(F32))
        scr_ref[8:8 + tb, :] = cur_ref[0].astype(F32)
        scr_ref[8 + tb:16 + tb, :] = jnp.where(last, 0.0, next_ref[0].astype(F32))
        for r0 in range(0, tb, CHUNK):
            window = scr_ref[r0:r0 + win, :]
            shifted = jnp.dot(sel, window.astype(BF16), preferred_element_type=F32)
            acc = b_ref[:, c0:c0 + w] + window[8:8 + CHUNK, :] * w_ref[pad:pad + 1, c0:c0 + w]
            for k, j in enumerate(taps):
                acc = acc + shifted[k * CHUNK:(k + 1) * CHUNK, :] * w_ref[j:j + 1, c0:c0 + w]
            out_ref[0, r0:r0 + CHUNK, :] = (acc * _sigmoid(acc)).astype(out_ref.dtype)

    run(xc_ref, xp_ref, xn_ref, sx_ref, ox_ref, 0)
    run(bc_ref, bp_ref, bn_ref, sbc_ref, obc_ref, B_INNER)


def _conv(main3, conv_w, conv_b, off, li):
    bsz, seq, _ = main3.shape
    tb = _tile(seq, 512)
    nb = seq // tb
    r8 = tb // 8
    last8 = seq // 8 - 1

    def trio(name, width):
        blk = off[name] // width
        return [pl.BlockSpec((1, tb, width), lambda b, i: (b, i, blk)),
                pl.BlockSpec((1, 8, width), lambda b, i: (b, jnp.maximum(i * r8 - 1, 0), blk)),
                pl.BlockSpec((1, 8, width), lambda b, i: (b, jnp.minimum((i + 1) * r8, last8), blk))]

    wtot = B_INNER + B_BC_W
    return pl.pallas_call(
        _conv_kernel,
        out_shape=(jax.ShapeDtypeStruct((bsz, seq, B_INNER), BF16),
                   jax.ShapeDtypeStruct((bsz, seq, B_BC_W), BF16)),
        grid=(bsz, nb),
        in_specs=trio("b_x", B_INNER) + trio("b_bc", B_BC_W) + [
            pl.BlockSpec((None, CONV_K, wtot), lambda b, i: (li, 0, 0)),
            pl.BlockSpec((None, 1, wtot), lambda b, i: (li, 0, 0))],
        out_specs=(pl.BlockSpec((1, tb, B_INNER), lambda b, i: (b, i, 0)),
                   pl.BlockSpec((1, tb, B_BC_W), lambda b, i: (b, i, 0))),
        scratch_shapes=[pltpu.VMEM((tb + 16, B_INNER), F32), pltpu.VMEM((tb + 16, B_BC_W), F32)],
        compiler_params=_cparams("parallel", "parallel"),
        name="dwconv_silu",
    )(main3, main3, main3, main3, main3, main3, conv_w, conv_b)


def _ssd_kernel(*refs, reverse, final):
    if final:
        (xs_ref, bc_ref, gc_ref, gr_ref, alr_ref, alc_ref, z_ref, yf_ref, dsk_ref, ng_ref,
         out_ref, s_ref, y_ref) = refs
    else:
        xs_ref, bc_ref, gc_ref, gr_ref, alr_ref, alc_ref, out_ref, s_ref, y_ref = refs

    @pl.when(pl.program_id(1) == 0)
    def _():
        s_ref[...] = jnp.zeros_like(s_ref)

    d = 1 if reverse else 0
    col0 = DT_COL0 + d * B_HEADS
    allowed = _chunk_mask(reverse)
    amat = allowed.astype(F32)
    lane = lax.broadcasted_iota(jnp.int32, (1, GATE_LANES), 1)
    a_row = jnp.where((lane >= col0) & (lane < col0 + B_HEADS), -jnp.exp(alr_ref[...]), 0.0)
    row = lax.broadcasted_iota(jnp.int32, (N_GATE_ROWS, 1), 0)
    a_col = jnp.where((row >= col0) & (row < col0 + B_HEADS), -jnp.exp(alc_ref[...]), 0.0)
    er = lax.broadcasted_iota(jnp.int32, (GATE_LANES, B_INNER), 0)
    ec = lax.broadcasted_iota(jnp.int32, (GATE_LANES, B_INNER), 1)
    expand = (er == col0 + (ec >> (B_HEADDIM.bit_length() - 1))).astype(F32)
    expand_b = expand.astype(BF16)

    def spread(t):
        return jnp.dot(t.astype(BF16), expand_b, preferred_element_type=F32)

    gw = B_INNER // B_GROUPS
    hpg = B_HEADS // B_GROUPS
    n_sub = xs_ref.shape[1] // CHUNK
    s_st = [s_ref[g] for g in range(B_GROUPS)]
    for cc in (reversed(range(n_sub)) if reverse else range(n_sub)):
        rows = slice(cc * CHUNK, (cc + 1) * CHUNK)
        gc = gc_ref[0, rows, :]
        gr = gr_ref[:, rows]
        dta_c = gc * a_row
        dta_r = gr * a_col
        acs_c = jnp.dot(amat, dta_c, precision=HI, preferred_element_type=F32)
        acs_r = lax.dot_general(dta_r, amat, NT_DIMS, precision=HI, preferred_element_type=F32)
        tot = jnp.sum(dta_c, axis=0, keepdims=True)
        dt_x = spread(gc)
        to_end_x = spread(jnp.exp(tot - acs_c))
        from_start_x = spread(jnp.exp(acs_c))
        chunk_decay_x = jnp.dot(jnp.broadcast_to(jnp.exp(tot), (8, GATE_LANES)), expand, precision=HI,
                                preferred_element_type=F32)[0:1]
        xs = xs_ref[0, rows, :].astype(F32)
        xdt = xs * dt_x
        xdt_b = xdt.astype(BF16)
        xend_b = (xdt * to_end_x).astype(BF16)
        for g in range(B_GROUPS):
            bm = bc_ref[0, rows, g * B_STATE:(g + 1) * B_STATE]
            cm = bc_ref[0, rows, (B_GROUPS + g) * B_STATE:(B_GROUPS + g + 1) * B_STATE]
            gs = slice(g * gw, (g + 1) * gw)
            s0 = s_st[g]
            y_ref[rows, gs] = jnp.dot(cm, s0.astype(BF16), preferred_element_type=F32) * from_start_x[:, gs]
            states = lax.dot_general(bm, xend_b[:, gs], TN_DIMS, preferred_element_type=F32)
            s_st[g] = chunk_decay_x[:, gs] * s0 + states
            cb = lax.dot_general(cm, bm, NT_DIMS, preferred_element_type=F32)
            for hh in range(hpg):
                ci = col0 + g * hpg + hh
                seg = acs_c[:, ci:ci + 1] - acs_r[ci:ci + 1, :]
                dec = jnp.exp(jnp.where(allowed, seg, -jnp.inf))
                hs = slice((g * hpg + hh) * B_HEADDIM, (g * hpg + hh + 1) * B_HEADDIM)
                y_ref[rows, hs] += jnp.dot((cb * dec).astype(BF16), xdt_b[:, hs], preferred_element_type=F32)
    for g in range(B_GROUPS):
        s_ref[g] = s_st[g]
    if final:
        y = y_ref[...] + yf_ref[0] + dsk_ref[...] * xs_ref[0].astype(F32)
        zf = z_ref[0].astype(F32)
        out_ref[0] = _rms(y * (zf * _sigmoid(zf)), ng_ref[...]).astype(out_ref.dtype)
    else:
        out_ref[0] = y_ref[...]


def _ssd_dir(xs3, bc3, gc3, gr, alog_r, alog_c, reverse, li, main3=None, off=None, y_fwd=None, d_skip=None,
             norm_g=None):
    bsz, seq, _ = xs3.shape
    rows = SCAN_CHUNKS_PER_STEP * CHUNK
    nc = seq // rows
    final = y_fwd is not None

    def cidx(c):
        return (nc - 1 - c) if reverse else c

    def seq_spec(width, blk=0):
        return pl.BlockSpec((1, rows, width), lambda b, c: (b, cidx(c), blk))

    def const_spec(shape):
        return pl.BlockSpec((None,) + shape, lambda b, c: (li, 0, 0))

    in_specs = [seq_spec(B_INNER), seq_spec(B_BC_W), seq_spec(GATE_LANES),
                pl.BlockSpec((N_GATE_ROWS, rows), lambda b, c: (0, b * nc + cidx(c))),
                const_spec((1, GATE_LANES)), const_spec((N_GATE_ROWS, 1))]
    args = [xs3, bc3, gc3, gr, alog_r, alog_c]
    if final:
        in_specs += [seq_spec(B_INNER, off["b_z"] // B_INNER), seq_spec(B_INNER),
                     const_spec((1, B_INNER)), const_spec((1, B_INNER))]
        args += [main3, y_fwd, d_skip, norm_g]
    return pl.pallas_call(
        functools.partial(_ssd_kernel, reverse=reverse, final=final),
        out_shape=jax.ShapeDtypeStruct((bsz, seq, B_INNER), BF16 if final else F32),
        grid=(bsz, nc),
        in_specs=in_specs,
        out_specs=seq_spec(B_INNER),
        scratch_shapes=[pltpu.VMEM((B_GROUPS, B_STATE, B_INNER // B_GROUPS), F32),
                        pltpu.VMEM((rows, B_INNER), F32)],
        compiler_params=_cparams("parallel", "arbitrary"),
        name="ssd_bwd" if reverse else "ssd_fwd",
    )(*args)


def _rope_kernel(pos_ref, tab_ref):
    half = C_ROPE // 2
    lane = lax.broadcasted_iota(jnp.int32, (1, 4 * half), 1)
    fr = (lane & (half - 1)).astype(F32)
    inv_freq = jnp.exp(fr * (-math.log(ROPE_THETA) / half))
    ang = pos_ref[...].astype(F32) * inv_freq
    sn = jnp.sin(ang)
    tab_ref[...] = jnp.where(lane < 2 * half, jnp.cos(ang), jnp.where(lane < 3 * half, -sn, sn))


def _rope_table(pos_col):
    t = pos_col.shape[0]
    tm = _tile(t, 1024)
    return pl.pallas_call(
        _rope_kernel,
        out_shape=jax.ShapeDtypeStruct((t, 2 * C_ROPE), F32),
        grid=(t // tm,),
        in_specs=[pl.BlockSpec((tm, 1), lambda i: (i, 0))],
        out_specs=pl.BlockSpec((tm, 2 * C_ROPE), lambda i: (i, 0)),
        compiler_params=_cparams("parallel"),
        name="rope_table",
    )(pos_col)


def _mla_prep_kernel(cq_ref, ckv_ref, ckr_ref, tab_ref, qg_ref, kvg_ref, wq_ref, wkv_ref, q_ref, k_ref, v_ref):
    tab = tab_ref[0]
    lane = lax.broadcasted_iota(jnp.int32, tab.shape, 1)

    def rope(t, zero_upper):
        p = t * tab
        r = p + pltpu.roll(p, C_ROPE, axis=1)
        return jnp.where(lane < C_ROPE, r, 0.0) if zero_upper else r

    cqn = _rms(cq_ref[0].astype(F32), qg_ref[...]).astype(BF16)
    ckvn = _rms(ckv_ref[0].astype(F32), kvg_ref[...]).astype(BF16)
    q_all = jnp.dot(cqn, wq_ref[...], preferred_element_type=F32)
    kv_all = jnp.dot(ckvn, wkv_ref[...], preferred_element_type=F32)
    k_rot = rope(ckr_ref[0].astype(F32), False).astype(BF16)
    qscale = (C_NOPE + C_ROPE) ** -0.5 * math.log2(math.e)
    ones_col = (lane == 0).astype(BF16)
    for hd in range(C_HEADS):
        q0 = hd * C_QK_PAD
        q_ref[0, hd, :, 0:C_NOPE] = (q_all[:, q0:q0 + C_NOPE] * qscale).astype(BF16)
        q_ref[0, hd, :, C_NOPE:C_QK_PAD] = (rope(q_all[:, q0 + C_NOPE:q0 + C_QK_PAD], True) * qscale).astype(BF16)
        k_ref[0, hd, :, 0:C_NOPE] = kv_all[:, hd * C_NOPE:(hd + 1) * C_NOPE].astype(BF16)
        k_ref[0, hd, :, C_NOPE:C_QK_PAD] = k_rot
        v0 = C_HEADS * C_NOPE + hd * C_V
        v_ref[0, hd, :, 0:C_V] = kv_all[:, v0:v0 + C_V].astype(BF16)
        v_ref[0, hd, :, C_V:C_V_PAD] = ones_col


def _mla_prep(main3, tab3, q_g, kv_g, wq, wkv, off, li):
    bsz, seq, _ = main3.shape
    tm = _tile(seq, 512)

    def col(name, width):
        blk = off[name] // width
        return pl.BlockSpec((1, tm, width), lambda b, i: (b, i, blk))

    def const_spec(shape):
        return pl.BlockSpec((None,) + tuple(shape), lambda b, i: (li, 0, 0))

    def head_spec(width):
        return pl.BlockSpec((1, C_HEADS, tm, width), lambda b, i: (b, 0, i, 0))

    return pl.pallas_call(
        _mla_prep_kernel,
        out_shape=(jax.ShapeDtypeStruct((bsz, C_HEADS, seq, C_QK_PAD), BF16),
                   jax.ShapeDtypeStruct((bsz, C_HEADS, seq, C_QK_PAD), BF16),
                   jax.ShapeDtypeStruct((bsz, C_HEADS, seq, C_V_PAD), BF16)),
        grid=(bsz, seq // tm),
        in_specs=[col("c_q", C_Q_RANK), col("c_kv", C_KV_RANK), col("c_kr", 2 * C_ROPE),
                  pl.BlockSpec((1, tm, 2 * C_ROPE), lambda b, i: (b, i, 0)),
                  const_spec((1, C_Q_RANK)), const_spec((1, C_KV_RANK)),
                  const_spec(wq.shape[1:]), const_spec(wkv.shape[1:])],
        out_specs=(head_spec(C_QK_PAD), head_spec(C_QK_PAD), head_spec(C_V_PAD)),
        compiler_params=_cparams("parallel", "parallel"),
        name="mla_prep",
    )(main3, main3, main3, tab3, q_g, kv_g, wq, wkv)


def _attn_kernel(q_ref, k_ref, v_ref, o_ref, *, tk):
    q = q_ref[0, 0]
    nk = k_ref.shape[2] // tk
    m = None
    acc = None
    for j in range(nk):
        kb = k_ref[0, 0, j * tk:(j + 1) * tk, :]
        vb = v_ref[0, 0, j * tk:(j + 1) * tk, :]
        s = lax.dot_general(q, kb, NT_DIMS, preferred_element_type=F32)
        m_blk = jnp.max(s, axis=1, keepdims=True)
        m_new = m_blk if m is None else jnp.maximum(m, m_blk)
        pv = jnp.dot(jnp.exp2(s - m_new).astype(BF16), vb, preferred_element_type=F32)
        acc = pv if acc is None else jnp.exp2(m - m_new) * acc + pv
        m = m_new
    o_ref[0] = (acc[:, 0:C_V] / acc[:, C_V:C_V + 1]).astype(o_ref.dtype)


def _attention(q4, k4, v4):
    bsz, nh, seq, _ = q4.shape
    tq = _tile(seq, 1024)
    tk = _tile(seq, 256)
    return pl.pallas_call(
        functools.partial(_attn_kernel, tk=tk),
        out_shape=jax.ShapeDtypeStruct((bsz, seq, nh * C_V), BF16),
        grid=(bsz, nh, seq // tq),
        in_specs=[pl.BlockSpec((1, 1, tq, C_QK_PAD), lambda b, h, i: (b, h, i, 0)),
                  pl.BlockSpec((1, 1, seq, C_QK_PAD), lambda b, h, i: (b, h, 0, 0)),
                  pl.BlockSpec((1, 1, seq, C_V_PAD), lambda b, h, i: (b, h, 0, 0))],
        out_specs=pl.BlockSpec((1, tq, C_V), lambda b, h, i: (b, i, h)),
        compiler_params=_cparams("parallel", "parallel", "parallel"),
        name="mla_attention",
    )(q4, k4, v4)


def _merge_kernel(ya_ref, yb_ref, yc_ref, wb_ref, g0_ref, g1_ref, g2_ref, o_ref):
    acc = None
    for y_ref, g_ref, k in ((ya_ref, g0_ref, 0), (yb_ref, g1_ref, 1), (yc_ref, g2_ref, 2)):
        t = _sigmoid(g_ref[...].astype(F32)) * jnp.dot(y_ref[...], wb_ref[k], preferred_element_type=F32)
        acc = t if acc is None else acc + t
    o_ref[...] = acc.astype(o_ref.dtype)


def _merge(ya, yb, yc, wb, main, off, d_model, li):
    t = ya.shape[0]
    tm = _tile(t, 1024)
    tn = _tile(d_model, 1024)
    gblk = off["gate"] // tn
    per = d_model // tn

    def gate_spec(k):
        return pl.BlockSpec((tm, tn), lambda i, j: (i, gblk + k * per + j))

    y_spec = pl.BlockSpec((tm, BRANCH_W), lambda i, j: (i, 0))
    return pl.pallas_call(
        _merge_kernel,
        out_shape=jax.ShapeDtypeStruct((t, d_model), BF16),
        grid=(t // tm, per),
        in_specs=[y_spec, y_spec, y_spec,
                  pl.BlockSpec((None, N_BRANCH, BRANCH_W, tn), lambda i, j: (li, 0, 0, j)),
                  gate_spec(0), gate_spec(1), gate_spec(2)],
        out_specs=pl.BlockSpec((tm, tn), lambda i, j: (i, j)),
        compiler_params=_cparams("parallel", "parallel"),
        name="branch_merge",
    )(ya, yb, yc, wb, main, main, main)


def _mm_res_kernel(a_ref, w_ref, r_ref, o_ref):
    o_ref[...] = r_ref[...] + jnp.dot(a_ref[...], w_ref[...], preferred_element_type=F32)


def _mm_res(a, w, res, li):
    t, k = a.shape
    n = w.shape[2]
    tm = _tile(t, 1024)
    tn = _tile(n, 1024)
    return pl.pallas_call(
        _mm_res_kernel,
        out_shape=jax.ShapeDtypeStruct((t, n), F32),
        grid=(t // tm, n // tn),
        in_specs=[pl.BlockSpec((tm, k), lambda i, j: (i, 0)),
                  pl.BlockSpec((None, k, tn), lambda i, j: (li, 0, j)),
                  pl.BlockSpec((tm, tn), lambda i, j: (i, j))],
        out_specs=pl.BlockSpec((tm, tn), lambda i, j: (i, j)),
        compiler_params=_cparams("parallel", "parallel"),
        name="out_proj",
    )(a, w, res)


def _ple_kernel(x_ref, g_ref, wg_ref, p_ref, wp_ref, *rest, last):
    x = x_ref[...]
    xn = _rms(x, g_ref[...]).astype(BF16)
    gate = _sigmoid(jnp.dot(xn, wg_ref[...], preferred_element_type=F32))
    emb = jnp.dot(p_ref[...].astype(BF16), wp_ref[...], preferred_element_type=F32)
    h_new = x + gate * emb
    if last:
        fg_ref, o_ref = rest
        o_ref[...] = _rms(h_new, fg_ref[...])
    else:
        rest[0][...] = h_new


def _ple(h, g, wgate, p, wproj, li, final_g=None):
    t, d = h.shape
    pd = p.shape[2]
    tm = _tile(t, 512)
    last = final_g is not None
    in_specs = [pl.BlockSpec((tm, d), lambda i: (i, 0)),
                pl.BlockSpec((None, 1, d), lambda i: (li, 0, 0)),
                pl.BlockSpec((None, d, d), lambda i: (li, 0, 0)),
                pl.BlockSpec((None, tm, pd), lambda i: (li, i, 0)),
                pl.BlockSpec((None, pd, d), lambda i: (li, 0, 0))]
    args = [h, g, wgate, p, wproj]
    if last:
        in_specs.append(pl.BlockSpec((1, d), lambda i: (0, 0)))
        args.append(final_g)
    return pl.pallas_call(
        functools.partial(_ple_kernel, last=last),
        out_shape=jax.ShapeDtypeStruct((t, d), F32),
        grid=(t // tm,),
        in_specs=in_specs,
        out_specs=pl.BlockSpec((tm, d), lambda i: (i, 0)),
        compiler_params=_cparams("parallel"),
        name="ple",
    )(*args)


def _prep_w_in(w_in, d_model):
    sizes = (A_QK_W, A_QK_W, A_V_W, A_V_W, 2 * A_HEADS, 2 * A_HEADS, B_INNER, B_INNER + B_BC_W,
             2 * B_HEADS, C_Q_RANK, C_KV_RANK, C_ROPE, N_BRANCH * d_model)
    w_t = jnp.swapaxes(w_in, 1, 2)
    cuts, o = [], 0
    for s in sizes:
        cuts.append((o, o + s))
        o += s
    (a_q, a_k, a_v, a_o, a_ig, a_fg, b_z, b_xbc, b_dt, c_q, c_kv, c_kr, gate) = [w_t[:, lo:hi] for lo, hi in cuts]
    half = C_ROPE // 2
    pieces = {"a_v": a_v, "a_o": a_o, "b_z": b_z, "b_x": b_xbc[:, :B_INNER], "gate": gate, "a_q": a_q,
              "a_k": a_k, "b_bc": b_xbc[:, B_INNER:], "c_q": c_q, "c_kv": c_kv,
              "c_kr": jnp.concatenate([c_kr, c_kr[:, half:], c_kr[:, :half]], axis=1)}
    off, n1 = _layout(d_model)
    order = sorted(off, key=off.get)
    n_pad = -n1 % 1024
    rows = [pieces[k] for k in order]
    if n_pad:
        rows.append(jnp.zeros((w_t.shape[0], n_pad, w_t.shape[2]), w_t.dtype))
    w_main = jnp.concatenate(rows, axis=1).astype(BF16)
    w_gate = jnp.concatenate([a_ig, a_fg, b_dt,
                              jnp.zeros((w_t.shape[0], GATE_LANES - N_GATE_ROWS, w_t.shape[2]), w_t.dtype)],
                             axis=1).astype(BF16)
    return w_main, w_gate, off


def _prep_mla(w_uq, w_ukv):
    nl, rq, _ = w_uq.shape
    half = C_ROPE // 2
    wq = w_uq.reshape(nl, rq, C_HEADS, C_NOPE + C_ROPE)
    t1 = wq[..., C_NOPE:C_NOPE + half]
    t2 = wq[..., C_NOPE + half:]
    wq = jnp.concatenate([wq[..., :C_NOPE], t1, t2, t2, t1], axis=-1).reshape(nl, rq, C_HEADS * C_QK_PAD)
    wkv = w_ukv.reshape(nl, w_ukv.shape[1], C_HEADS, C_NOPE + C_V)
    wkv = jnp.concatenate([wkv[..., :C_NOPE].reshape(nl, -1, C_HEADS * C_NOPE),
                           wkv[..., C_NOPE:].reshape(nl, -1, C_HEADS * C_V)], axis=-1)
    return wq.astype(BF16), wkv.astype(BF16)


def kernel(x, p, positions, ffn1_norm, ffn1_w13, ffn1_w2, mix_norm, w_in, mlstm_b_igate, mlstm_b_fgate, mlstm_norm, conv_w, conv_b, ssm_a_log, ssm_dt_bias, ssm_d, ssm_norm, mla_q_norm, mla_kv_norm, mla_w_uq, mla_w_ukv, w_branch, w_out, ffn2_norm, ffn2_w13, ffn2_w2, ple_norm, w_ple_gate, w_ple_proj, final_norm):
    bsz, seq, d = x.shape
    depth = w_in.shape[0]
    t = bsz * seq
    nl = depth

    w_main, w_gate, off = _prep_w_in(w_in, d)
    wq, wkv = _prep_mla(mla_w_uq, mla_w_ukv)
    ffn1_w13b, ffn1_w2b = ffn1_w13.astype(BF16), ffn1_w2.astype(BF16)
    ffn2_w13b, ffn2_w2b = ffn2_w13.astype(BF16), ffn2_w2.astype(BF16)
    w_branch_b, w_out_b = w_branch.astype(BF16), w_out.astype(BF16)
    w_pg_b, w_pp_b = w_ple_gate.astype(BF16), w_ple_proj.astype(BF16)
    gate_bias = jnp.concatenate([mlstm_b_igate.reshape(nl, -1), mlstm_b_fgate.reshape(nl, -1),
                                 ssm_dt_bias.reshape(nl, -1)], axis=-1).astype(F32)
    bias_c = jnp.pad(gate_bias, ((0, 0), (0, GATE_LANES - N_GATE_ROWS)))[:, None, :]
    bias_r = gate_bias[:, :, None]
    alog = ssm_a_log.reshape(nl, -1).astype(F32)
    alog_r = jnp.pad(alog, ((0, 0), (DT_COL0, GATE_LANES - N_GATE_ROWS)))[:, None, :]
    alog_c = jnp.pad(alog, ((0, 0), (DT_COL0, 0)))[:, :, None]
    d_skip = jnp.repeat(ssm_d.astype(F32), B_HEADDIM, axis=-1)[:, None, :]
    row = lambda v: v[:, None, :]
    p3 = p.reshape(nl, t, -1)

    tab3 = _rope_table(positions.reshape(t, 1)).reshape(bsz, seq, 2 * C_ROPE)

    h = x.reshape(t, d)
    for i in range(depth):
        h = _ffn(h, row(ffn1_norm), ffn1_w13b, ffn1_w2b, i)
        main, gc, gr = _proj(h, row(mix_norm), w_main, w_gate, bias_c, bias_r, i)
        main3 = main.reshape(bsz, seq, -1)
        gc3 = gc.reshape(bsz, seq, GATE_LANES)
        hf = _mlstm_dir(main3, gc3, gr, off, False)
        ya = _mlstm_dir(main3, gc3, gr, off, True, h_fwd=hf, norm_g=row(mlstm_norm), li=i)
        xs3, bc3 = _conv(main3, conv_w, row(conv_b), off, i)
        yf = _ssd_dir(xs3, bc3, gc3, gr, alog_r, alog_c, False, i)
        yb = _ssd_dir(xs3, bc3, gc3, gr, alog_r, alog_c, True, i, main3=main3, off=off, y_fwd=yf,
                      d_skip=d_skip, norm_g=row(ssm_norm))
        q4, k4, v4 = _mla_prep(main3, tab3, row(mla_q_norm), row(mla_kv_norm), wq, wkv, off, i)
        yc = _attention(q4, k4, v4)
        merged = _merge(ya.reshape(t, -1), yb.reshape(t, -1), yc.reshape(t, -1), w_branch_b, main, off, d, i)
        h = _mm_res(merged, w_out_b, h, i)
        h = _ffn(h, row(ffn2_norm), ffn2_w13b, ffn2_w2b, i)
        h = _ple(h, row(ple_norm), w_pg_b, p3, w_pp_b, i, final_g=final_norm[None] if i == depth - 1 else None)
    return h.reshape(bsz, seq, d)
```
